```python
import math
import jax, jax.numpy as jnp
from jax import lax
import numpy as np

D_MODEL = 2048
BATCH = 1
SEQ = 8192
DEPTH = 2
DEC_BATCH = 128
DEC_SEQ = 4
PAST_LEN = 2048
PAGE_SIZE = 128

D_MIX = D_MODEL
HEAD_DIM = 128
D_ATTN = D_MIX // 2
N_HEADS = D_ATTN // HEAD_DIM
D_CONV = D_MIX // 4
CONV_WIDTH = 31
D_LRU = D_MIX - D_ATTN - D_CONV
LRU_BLOCKS = 4
LRU_BLOCK = D_LRU // LRU_BLOCKS
LRU_CONV_WIDTH = 4
LRU_C = 8.0
QBLOCK = 128
EPS = 1e-6
FORGET_BIAS = 3.0
SPLIT_SIZES = (D_ATTN, D_ATTN, D_ATTN, N_HEADS, D_ATTN, D_CONV, D_CONV, D_CONV, D_LRU, D_LRU)
D_IN = D_ATTN * 4 + N_HEADS + D_CONV * 3 + D_LRU * 2

kernel_name = 'fox_conformer_rglru_hybrid_step'


def rmsnorm(x, g):
    xf = x.astype(jnp.float32)
    y = xf * lax.rsqrt(jnp.mean(xf * xf, axis=-1, keepdims=True) + EPS)
    return (y * g.astype(jnp.float32)).astype(x.dtype)


def layernorm(x, g, b):
    xf = x.astype(jnp.float32)
    mu = jnp.mean(xf, axis=-1, keepdims=True)
    xc = xf - mu
    y = xc * lax.rsqrt(jnp.mean(xc * xc, axis=-1, keepdims=True) + EPS)
    return (y * g.astype(jnp.float32) + b.astype(jnp.float32)).astype(x.dtype)


def causal_dwconv(x_ext, w, b):
    c = x_ext.shape[-1]
    y = lax.conv_general_dilated(x_ext, w[:, None, :].astype(x_ext.dtype), window_strides=(1,),
                                 padding='VALID', dimension_numbers=('NWC', 'WIO', 'NWC'),
                                 feature_group_count=c)
    return y + b.astype(y.dtype)


def blockdiag(x, w):
    bsz, length, c = x.shape
    nb, bs, _ = w.shape
    return jnp.einsum('blnk,nkj->blnj', x.reshape(bsz, length, nb, bs), w).reshape(bsz, length, c)


def project(x, norm_g, w_in, b_f, q_norm_g, k_norm_g):
    bsz, length, _ = x.shape
    z = rmsnorm(x, norm_g) @ w_in
    idx = []
    acc = 0
    for s in SPLIT_SIZES[:-1]:
        acc += s
        idx.append(acc)
    q, k, v, f, g_attn, glu_a, glu_b, g_conv, lru_x, g_lru = jnp.split(z, idx, axis=-1)
    q = rmsnorm(q.reshape(bsz, length, N_HEADS, HEAD_DIM), q_norm_g)
    k = rmsnorm(k.reshape(bsz, length, N_HEADS, HEAD_DIM), k_norm_g)
    v = v.reshape(bsz, length, N_HEADS, HEAD_DIM)
    logf = jax.nn.log_sigmoid((f + b_f).astype(jnp.float32))
    u = glu_a * jax.nn.sigmoid(glu_b)
    return q, k, v, logf, g_attn, u, g_conv, lru_x, g_lru


def fox_attn_prompt(q, k, v, c):
    bsz, s_len, h, d = q.shape
    nb = s_len // QBLOCK
    scale = HEAD_DIM ** -0.5
    qb = q.reshape(bsz, nb, QBLOCK, h, d).transpose(1, 0, 2, 3, 4)
    cb = c.reshape(bsz, nb, QBLOCK, h).transpose(1, 0, 2, 3)
    c_k = jnp.swapaxes(c, 1, 2)
    kpos = jnp.arange(s_len)

    def one_block(args):
        qi, ci, i = args
        s = jnp.einsum('bqhd,bkhd->bhqk', qi, k, preferred_element_type=jnp.float32) * scale
        s = s + jnp.swapaxes(ci, 1, 2)[..., None] - c_k[:, :, None, :]
        qpos = i * QBLOCK + jnp.arange(QBLOCK)
        s = jnp.where(kpos[None, :] <= qpos[:, None], s, -jnp.inf)
        p = jax.nn.softmax(s, axis=-1)
        return jnp.einsum('bhqk,bkhd->bqhd', p.astype(v.dtype), v)

    o = lax.map(one_block, (qb, cb, jnp.arange(nb)))
    return o.transpose(1, 0, 2, 3, 4).reshape(bsz, s_len, h, d)


def fox_attn_sample(q, k, v, c, k_past, v_past, c_past):
    t_len = q.shape[1]
    p_len = k_past.shape[1]
    scale = HEAD_DIM ** -0.5
    ct = jnp.swapaxes(c, 1, 2)
    s_p = jnp.einsum('bthd,bshd->bhts', q, k_past, preferred_element_type=jnp.float32) * scale
    s_p = s_p + ct[..., None] - jnp.swapaxes(c_past, 1, 2)[:, :, None, :]
    s_n = jnp.einsum('bthd,bshd->bhts', q, k, preferred_element_type=jnp.float32) * scale
    s_n = s_n + ct[..., None] - ct[:, :, None, :]
    causal = jnp.arange(t_len)[None, :] <= jnp.arange(t_len)[:, None]
    s_n = jnp.where(causal, s_n, -jnp.inf)
    p = jax.nn.softmax(jnp.concatenate([s_p, s_n], axis=-1), axis=-1).astype(v.dtype)
    return (jnp.einsum('bhts,bshd->bthd', p[..., :p_len], v_past)
            + jnp.einsum('bhts,bshd->bthd', p[..., p_len:], v))


def conv_branch(u_ext, dw_w, dw_b, ln_g, ln_b, pw_w):
    y = causal_dwconv(u_ext, dw_w, dw_b)
    y = jax.nn.silu(layernorm(y, ln_g, ln_b))
    return y @ pw_w


def rglru_branch(x_ext, h0, conv_w, conv_b, w_a, b_a, w_x, b_x, lam):
    xc = causal_dwconv(x_ext, conv_w, conv_b)
    r = jax.nn.sigmoid(blockdiag(xc, w_a) + b_a)
    i = jax.nn.sigmoid(blockdiag(xc, w_x) + b_x)
    log_a = -LRU_C * r.astype(jnp.float32) * jax.nn.softplus(-lam.astype(jnp.float32))
    a = jnp.exp(log_a)
    b = jnp.sqrt(-jnp.expm1(2.0 * log_a)) * (i * xc).astype(jnp.float32)
    b = b.at[:, 0].add(a[:, 0] * h0.astype(jnp.float32))

    def combine(e1, e2):
        a1, b1 = e1
        a2, b2 = e2
        return a1 * a2, a2 * b1 + b2

    _, h = lax.associative_scan(combine, (a, b), axis=1)
    return h.astype(xc.dtype), h[:, -1].astype(h0.dtype)


def merge(x, o_attn, g_attn, o_conv, g_conv, o_lru, g_lru, w_out):
    bsz, length, _ = x.shape
    y = jnp.concatenate([o_attn.reshape(bsz, length, D_ATTN) * jax.nn.silu(g_attn),
                         o_conv * jax.nn.silu(g_conv),
                         o_lru * jax.nn.silu(g_lru)], axis=-1)
    return x + y @ w_out


def setup_inputs(seed: int = 0) -> dict:
    key = jax.random.key(seed)
    ks = jax.random.split(key, 32)
    f32 = jnp.float32
    n_pages = PAST_LEN // PAGE_SIZE
    n_used = DEC_BATCH * n_pages
    n_pool = n_used + max(1, n_used // 4)

    def nrm(k, shape, s):
        return jax.random.normal(k, shape, f32) * s

    u = jax.random.uniform(ks[25], (DEPTH, D_LRU), f32, 0.9, 0.999)
    a_base = u ** (1.0 / LRU_C)
    lam = jnp.log(a_base / (1.0 - a_base))
    perm = jax.random.permutation(ks[8], n_pool)
    return {
        'x_prompt': nrm(ks[0], (BATCH, SEQ, D_MODEL), 1.0),
        'x_sample': nrm(ks[1], (DEC_BATCH, DEC_SEQ, D_MODEL), 1.0),
        'cache_k': nrm(ks[2], (DEPTH, n_pool, PAGE_SIZE, N_HEADS, HEAD_DIM), 1.0),
        'cache_v': nrm(ks[3], (DEPTH, n_pool, PAGE_SIZE, N_HEADS, HEAD_DIM), 1.0),
        'cache_logf': jax.nn.log_sigmoid(FORGET_BIAS + jax.random.normal(ks[4], (DEPTH, n_pool, PAGE_SIZE, N_HEADS), f32)),
        'state_conv': nrm(ks[5], (DEPTH, DEC_BATCH, CONV_WIDTH - 1, D_CONV), 1.0),
        'state_lru_conv': nrm(ks[6], (DEPTH, DEC_BATCH, LRU_CONV_WIDTH - 1, D_LRU), 1.0),
        'state_lru_h': nrm(ks[7], (DEPTH, DEC_BATCH, D_LRU), 0.5),
        'page_table': perm[:n_used].reshape(DEC_BATCH, n_pages).astype(jnp.int32),
        'norm_g': 1.0 + nrm(ks[9], (DEPTH, D_MODEL), 0.1),
        'w_in': nrm(ks[10], (DEPTH, D_MODEL, D_IN), D_MODEL ** -0.5),
        'b_f': FORGET_BIAS + nrm(ks[11], (DEPTH, N_HEADS), 0.5),
        'q_norm_g': 1.0 + nrm(ks[12], (DEPTH, HEAD_DIM), 0.1),
        'k_norm_g': 1.0 + nrm(ks[13], (DEPTH, HEAD_DIM), 0.1),
        'conv_dw_w': nrm(ks[14], (DEPTH, CONV_WIDTH, D_CONV), CONV_WIDTH ** -0.5),
        'conv_dw_b': nrm(ks[15], (DEPTH, D_CONV), 0.02),
        'conv_ln_g': 1.0 + nrm(ks[16], (DEPTH, D_CONV), 0.1),
        'conv_ln_b': nrm(ks[17], (DEPTH, D_CONV), 0.02),
        'conv_pw_w': nrm(ks[18], (DEPTH, D_CONV, D_CONV), D_CONV ** -0.5),
        'lru_conv_w': nrm(ks[19], (DEPTH, LRU_CONV_WIDTH, D_LRU), LRU_CONV_WIDTH ** -0.5),
        'lru_conv_b': nrm(ks[20], (DEPTH, D_LRU), 0.02),
        'lru_w_a': nrm(ks[21], (DEPTH, LRU_BLOCKS, LRU_BLOCK, LRU_BLOCK), LRU_BLOCK ** -0.5),
        'lru_b_a': nrm(ks[22], (DEPTH, D_LRU), 0.1),
        'lru_w_x': nrm(ks[23], (DEPTH, LRU_BLOCKS, LRU_BLOCK, LRU_BLOCK), LRU_BLOCK ** -0.5),
        'lru_b_x': nrm(ks[24], (DEPTH, D_LRU), 0.1),
        'lru_lambda': lam,
        'w_out': nrm(ks[26], (DEPTH, D_MIX, D_MODEL), D_MIX ** -0.5),
    }


def reference(x_prompt, x_sample, cache_k, cache_v, cache_logf, state_conv, state_lru_conv, state_lru_h,
              page_table, norm_g, w_in, b_f, q_norm_g, k_norm_g, conv_dw_w, conv_dw_b, conv_ln_g, conv_ln_b,
              conv_pw_w, lru_conv_w, lru_conv_b, lru_w_a, lru_b_a, lru_w_x, lru_b_x, lru_lambda, w_out):
    n_seq_s = page_table.shape[0]
    past = page_table.shape[1] * PAGE_SIZE
    n_seq_p = x_prompt.shape[0]
    xp, xs = x_prompt, x_sample
    pk, pv, pf, pc, plc, ph = [], [], [], [], [], []
    sk, sv, sf, sc, slc, sh = [], [], [], [], [], []
    for l in range(DEPTH):
        q, k, v, logf, ga, u, gc, ux, gl = project(xp, norm_g[l], w_in[l], b_f[l], q_norm_g[l], k_norm_g[l])
        c = jnp.cumsum(logf, axis=1)
        o_att = fox_attn_prompt(q, k, v, c)
        u_ext = jnp.concatenate([jnp.zeros((n_seq_p, CONV_WIDTH - 1, D_CONV), u.dtype), u], axis=1)
        o_conv = conv_branch(u_ext, conv_dw_w[l], conv_dw_b[l], conv_ln_g[l], conv_ln_b[l], conv_pw_w[l])
        ux_ext = jnp.concatenate([jnp.zeros((n_seq_p, LRU_CONV_WIDTH - 1, D_LRU), ux.dtype), ux], axis=1)
        o_lru, h_last = rglru_branch(ux_ext, jnp.zeros((n_seq_p, D_LRU), jnp.float32), lru_conv_w[l], lru_conv_b[l],
                                     lru_w_a[l], lru_b_a[l], lru_w_x[l], lru_b_x[l], lru_lambda[l])
        xp = merge(xp, o_att, ga, o_conv, gc, o_lru, gl, w_out[l])
        pk.append(k)
        pv.append(v)
        pf.append(logf)
        pc.append(u_ext[:, -(CONV_WIDTH - 1):])
        plc.append(ux_ext[:, -(LRU_CONV_WIDTH - 1):])
        ph.append(h_last)
        q, k, v, logf, ga, u, gc, ux, gl = project(xs, norm_g[l], w_in[l], b_f[l], q_norm_g[l], k_norm_g[l])
        k_past = cache_k[l][page_table].reshape(n_seq_s, past, N_HEADS, HEAD_DIM)
        v_past = cache_v[l][page_table].reshape(n_seq_s, past, N_HEADS, HEAD_DIM)
        c_past = jnp.cumsum(cache_logf[l][page_table].reshape(n_seq_s, past, N_HEADS).astype(jnp.float32), axis=1)
        c = c_past[:, -1:, :] + jnp.cumsum(logf, axis=1)
        o_att = fox_attn_sample(q, k.astype(k_past.dtype), v.astype(v_past.dtype), c, k_past, v_past, c_past)
        u_ext = jnp.concatenate([state_conv[l].astype(u.dtype), u], axis=1)
        o_conv = conv_branch(u_ext, conv_dw_w[l], conv_dw_b[l], conv_ln_g[l], conv_ln_b[l], conv_pw_w[l])
        ux_ext = jnp.concatenate([state_lru_conv[l].astype(ux.dtype), ux], axis=1)
        o_lru, h_last = rglru_branch(ux_ext, state_lru_h[l], lru_conv_w[l], lru_conv_b[l],
                                     lru_w_a[l], lru_b_a[l], lru_w_x[l], lru_b_x[l], lru_lambda[l])
        xs = merge(xs, o_att, ga, o_conv, gc, o_lru, gl, w_out[l])
        sk.append(k)
        sv.append(v)
        sf.append(logf)
        sc.append(u_ext[:, -(CONV_WIDTH - 1):])
        slc.append(ux_ext[:, -(LRU_CONV_WIDTH - 1):])
        sh.append(h_last)
    return (xp, xs,
            jnp.stack(pk), jnp.stack(pv), jnp.stack(pf), jnp.stack(pc), jnp.stack(plc), jnp.stack(ph),
            jnp.stack(sk), jnp.stack(sv), jnp.stack(sf), jnp.stack(sc), jnp.stack(slc), jnp.stack(sh))
```

```python
import functools

import jax
import jax.numpy as jnp
from jax import lax
from jax.experimental import pallas as pl
from jax.experimental.pallas import tpu as pltpu

f32 = jnp.float32
bf16 = jnp.bfloat16

D_MODEL = 2048
HEAD_DIM = 128
N_HEADS = 8
D_ATTN = N_HEADS * HEAD_DIM
D_CONV = 512
CONV_WIDTH = 31
D_LRU = 512
LRU_BLOCKS = 4
LRU_BLOCK = D_LRU // LRU_BLOCKS
LRU_CONV_WIDTH = 4
LRU_C = 8.0
PAGE_SIZE = 128
EPS = 1e-6
SCALE = HEAD_DIM ** -0.5

LANE = 128
SUBLANE = 8
VMEM_LIMIT = 48 * 1024 * 1024

PROJ_TN = 512
Z_Q, Z_K, Z_V, Z_GA, Z_GC, Z_LX, Z_GL, Z_SIG, Z_U = 0, 2, 4, 6, 8, 9, 10, 11, 12
Z_COLS = 13 * PROJ_TN


def _cparams(*sem):
    return pltpu.CompilerParams(dimension_semantics=sem, vmem_limit_bytes=VMEM_LIMIT)


def _split3(x):
    hi = x.astype(bf16).astype(f32)
    r1 = x - hi
    mid = r1.astype(bf16).astype(f32)
    lo = (r1 - mid).astype(bf16).astype(f32)
    return hi, mid, lo


def _head_rmsnorm(z, g):
    outs = []
    for h in range(PROJ_TN // HEAD_DIM):
        zh = z[:, h * HEAD_DIM:(h + 1) * HEAD_DIM]
        ms = jnp.mean(zh * zh, axis=-1, keepdims=True)
        outs.append(zh * lax.rsqrt(ms + EPS) * g)
    return jnp.concatenate(outs, axis=1)


def _proj_kernel(x_ref, g_ref, wm_ref, wf_ref, bf_ref, qg_ref, kg_ref,
                 z_ref, qkv_ref, logf_ref, cs_ref,
                 xn_ref, sig_ref, carry_ref, *, seq_rows, carry_over):
    i = pl.program_id(0)
    j = pl.program_id(1)
    tm = x_ref.shape[0]

    @pl.when(j == 0)
    def _():
        x = x_ref[...]
        y = x * lax.rsqrt(jnp.mean(x * x, axis=-1, keepdims=True) + EPS) * g_ref[...]
        xn = y.astype(bf16)
        xn_ref[...] = xn
        f = jnp.dot(xn, wf_ref[...], preferred_element_type=f32) + bf_ref[...]
        logf = jax.nn.log_sigmoid(f)
        logf_ref[...] = logf
        parts = jnp.concatenate(_split3(logf), axis=1).astype(bf16)
        ii = lax.broadcasted_iota(jnp.int32, (tm, tm), 0)
        jj = lax.broadcasted_iota(jnp.int32, (tm, tm), 1)
        keep = jj <= ii
        if not carry_over:
            keep = jnp.logical_and(keep, (ii // seq_rows) == (jj // seq_rows))
        tri = jnp.where(keep, 1.0, 0.0).astype(bf16)
        c3 = jnp.dot(tri, parts, preferred_element_type=f32)
        cs = c3[:, :LANE] + c3[:, LANE:2 * LANE] + c3[:, 2 * LANE:]
        if carry_over:
            @pl.when(i == 0)
            def _():
                carry_ref[...] = jnp.zeros_like(carry_ref)
            cs = cs + carry_ref[...]
            carry_ref[...] = cs[tm - 1:tm, :]
        cs_ref[...] = cs

    z = jnp.dot(xn_ref[...], wm_ref[...], preferred_element_type=f32)

    @pl.when(j < Z_K)
    def _():
        qn = _head_rmsnorm(z, qg_ref[...])
        z_ref[...] = qn
        qkv_ref[...] = (qn * SCALE).astype(bf16)

    @pl.when(jnp.logical_and(j >= Z_K, j < Z_V))
    def _():
        kn = _head_rmsnorm(z, kg_ref[...])
        z_ref[...] = kn
        qkv_ref[...] = kn.astype(bf16)

    @pl.when(jnp.logical_and(j >= Z_V, j < Z_GA))
    def _():
        z_ref[...] = z
        qkv_ref[...] = z.astype(bf16)

    @pl.when(jnp.logical_and(j >= Z_GA, j < Z_SIG))
    def _():
        z_ref[...] = z

    @pl.when(j == Z_SIG)
    def _():
        s = jax.nn.sigmoid(z)
        sig_ref[...] = s
        z_ref[...] = s

    @pl.when(j == Z_U)
    def _():
        z_ref[...] = z * sig_ref[...]


def _project(x2d, norm_g, wm, wf, bfp, qg, kg, *, seq_rows, carry_over):
    n = x2d.shape[0]
    tm = min(512, n)
    assert n % tm == 0 and (carry_over or tm % seq_rows == 0)
    grid = (n // tm, Z_COLS // PROJ_TN)
    last_qkv = Z_GA - 1
    kern = functools.partial(_proj_kernel, seq_rows=seq_rows, carry_over=carry_over)
    return pl.pallas_call(
        kern,
        grid=grid,
        in_specs=[
            pl.BlockSpec((tm, D_MODEL), lambda i, j: (i, 0)),
            pl.BlockSpec((1, D_MODEL), lambda i, j: (0, 0)),
            pl.BlockSpec((D_MODEL, PROJ_TN), lambda i, j: (0, j)),
            pl.BlockSpec((D_MODEL, LANE), lambda i, j: (0, 0)),
            pl.BlockSpec((1, LANE), lambda i, j: (0, 0)),
            pl.BlockSpec((1, HEAD_DIM), lambda i, j: (0, 0)),
            pl.BlockSpec((1, HEAD_DIM), lambda i, j: (0, 0)),
        ],
        out_specs=[
            pl.BlockSpec((tm, PROJ_TN), lambda i, j: (i, j)),
            pl.BlockSpec((tm, PROJ_TN), lambda i, j: (i, jnp.minimum(j, last_qkv))),
            pl.BlockSpec((tm, LANE), lambda i, j: (i, 0)),
            pl.BlockSpec((tm, LANE), lambda i, j: (i, 0)),
        ],
        out_shape=[
            jax.ShapeDtypeStruct((n, Z_COLS), f32),
            jax.ShapeDtypeStruct((n, 3 * D_ATTN), bf16),
            jax.ShapeDtypeStruct((n, LANE), f32),
            jax.ShapeDtypeStruct((n, LANE), f32),
        ],
        scratch_shapes=[
            pltpu.VMEM((tm, D_MODEL), bf16),
            pltpu.VMEM((tm, PROJ_TN), f32),
            pltpu.VMEM((1, LANE), f32),
        ],
        compiler_params=_cparams("arbitrary", "arbitrary"),
        name="proj",
    )(x2d, norm_g, wm, wf, bfp, qg, kg)


def _attn_prompt_kernel(q_ref, k_ref, v_ref, crow_ref, ccol_ref, o_ref, *, tk):
    qi = pl.program_id(1)
    tq = q_ref.shape[0]
    q = q_ref[...]
    cq = ccol_ref[0]

    def step(kj, carry, masked):
        m, l, acc = carry
        k0 = pl.multiple_of(kj * tk, tk)
        k = k_ref[pl.ds(k0, tk), :]
        v = v_ref[pl.ds(k0, tk), :]
        s = lax.dot_general(q, k, (((1,), (1,)), ((), ())), preferred_element_type=f32)
        s = s - crow_ref[0, :, pl.ds(k0, tk)]
        if masked:
            row = qi * tq + lax.broadcasted_iota(jnp.int32, (tq, tk), 0)
            col = k0 + lax.broadcasted_iota(jnp.int32, (tq, tk), 1)
            s = jnp.where(col <= row, s, -jnp.inf)
        m_new = jnp.maximum(m, jnp.max(s, axis=-1, keepdims=True))
        shift = (m_new + cq) - cq
        alpha = jnp.exp(m - shift)
        p = jnp.exp(s - shift)
        l = alpha * l + jnp.sum(p, axis=-1, keepdims=True)
        acc = alpha * acc + jnp.dot(p.astype(bf16), v, preferred_element_type=f32)
        return shift, l, acc

    init = (jnp.full((tq, 1), -jnp.inf, f32), jnp.zeros((tq, 1), f32), jnp.zeros((tq, HEAD_DIM), f32))
    n_full = (qi * tq) // tk
    carry = lax.fori_loop(0, n_full, lambda kj, c: step(kj, c, False), init)
    for d in range(tq // tk):
        carry = step(n_full + d, carry, True)
    _, l, acc = carry
    o_ref[...] = acc / l


def _attn_prompt(qkv16, crow, ccol):
    s_len = qkv16.shape[0]
    tq = min(512, s_len)
    tk = tq
    assert s_len % tq == 0 and tq % tk == 0
    return pl.pallas_call(
        functools.partial(_attn_prompt_kernel, tk=tk),
        grid=(N_HEADS, s_len // tq),
        in_specs=[
            pl.BlockSpec((tq, HEAD_DIM), lambda h, i: (i, h)),
            pl.BlockSpec((s_len, HEAD_DIM), lambda h, i: (0, N_HEADS + h)),
            pl.BlockSpec((s_len, HEAD_DIM), lambda h, i: (0, 2 * N_HEADS + h)),
            pl.BlockSpec((1, 1, s_len), lambda h, i: (h, 0, 0)),
            pl.BlockSpec((1, tq, 1), lambda h, i: (h, i, 0)),
        ],
        out_specs=pl.BlockSpec((tq, HEAD_DIM), lambda h, i: (i, h)),
        out_shape=jax.ShapeDtypeStruct((s_len, D_ATTN), f32),
        compiler_params=_cparams("arbitrary", "arbitrary"),
        name="attn_prompt",
    )(qkv16, qkv16, qkv16, crow, ccol)


def _decode_kernel(pt_ref, z_q_ref, z_k_ref, z_v_ref, cscol_ref, csn_ref, *rest, pages_per_step, t_len):
    pp = pages_per_step
    k_refs = rest[:pp]
    v_refs = rest[pp:2 * pp]
    lf_refs = rest[2 * pp:3 * pp]
    o_ref = rest[3 * pp]
    qbd_ref, qbd16_ref, m_ref, l_ref, acc_ref, carry_ref = rest[3 * pp + 1:]
    g = pl.program_id(1)
    n_rows = t_len * N_HEADS
    head_of_col = lax.broadcasted_iota(jnp.int32, (N_HEADS, D_ATTN), 1) // HEAD_DIM
    head_of_row = lax.broadcasted_iota(jnp.int32, (N_HEADS, D_ATTN), 0)
    diag = head_of_col == head_of_row

    @pl.when(g == 0)
    def _():
        q = z_q_ref[0]
        slabs = [jnp.where(diag, jnp.broadcast_to(q[t:t + 1, :], (N_HEADS, D_ATTN)), 0.0) for t in range(t_len)]
        qbd = jnp.concatenate(slabs, axis=0) * SCALE
        qbd_ref[...] = qbd
        qbd16_ref[...] = qbd.astype(bf16)
        m_ref[...] = jnp.full_like(m_ref, -jnp.inf)
        l_ref[...] = jnp.zeros_like(l_ref)
        acc_ref[...] = jnp.zeros_like(acc_ref)
        carry_ref[...] = jnp.zeros_like(carry_ref)

    ii = lax.broadcasted_iota(jnp.int32, (PAGE_SIZE, PAGE_SIZE), 0)
    jj = lax.broadcasted_iota(jnp.int32, (PAGE_SIZE, PAGE_SIZE), 1)
    upper = jnp.where(ii <= jj, 1.0, 0.0).astype(bf16)

    def update(s, pv_fn):
        m = m_ref[...]
        m_new = jnp.maximum(m, jnp.max(s, axis=-1, keepdims=True))
        alpha = jnp.exp(m - m_new)
        p = jnp.exp(s - m_new)
        l_ref[...] = alpha * l_ref[...] + jnp.sum(p, axis=-1, keepdims=True)
        acc_ref[...] = alpha * acc_ref[...] + pv_fn(p)
        m_ref[...] = m_new

    for pg in range(pp):
        parts = jnp.concatenate(_split3(lf_refs[pg][0, 0]), axis=0).astype(bf16)
        c3 = jnp.dot(parts, upper, preferred_element_type=f32)
        c_page = (c3[:N_HEADS] + c3[N_HEADS:2 * N_HEADS] + c3[2 * N_HEADS:]) + carry_ref[...]
        carry_ref[...] = jnp.broadcast_to(c_page[:, PAGE_SIZE - 1:PAGE_SIZE], (N_HEADS, LANE))
        k16 = k_refs[pg][0, 0].astype(bf16)
        v16 = v_refs[pg][0, 0].astype(bf16)
        s = lax.dot_general(qbd16_ref[...], k16, (((1,), (1,)), ((), ())), preferred_element_type=f32)
        s = s - jnp.concatenate([c_page] * t_len, axis=0)
        update(s, lambda p: jnp.dot(p.astype(bf16), v16, preferred_element_type=f32))

    @pl.when(g == pl.num_programs(1) - 1)
    def _():
        c_last = jnp.concatenate([carry_ref[:, 0:1]] * t_len, axis=0)
        ct_row = c_last + cscol_ref[0]
        ct_new = c_last + csn_ref[0]
        qbd = qbd_ref[...]
        kn = z_k_ref[0]
        vn = z_v_ref[0]
        cols = [jnp.sum(qbd * kn[t:t + 1, :], axis=-1, keepdims=True) for t in range(t_len)]
        s_n = jnp.concatenate(cols, axis=1) - ct_new
        t_row = lax.broadcasted_iota(jnp.int32, (n_rows, t_len), 0) // N_HEADS
        t_col = lax.broadcasted_iota(jnp.int32, (n_rows, t_len), 1)
        s_n = jnp.where(t_col <= t_row, s_n, -jnp.inf)
        m = m_ref[...]
        m_new = jnp.maximum(m, jnp.max(s_n, axis=-1, keepdims=True))
        shift = (m_new + ct_row) - ct_row
        alpha = jnp.exp(m - shift)
        p = jnp.exp(s_n - shift)
        l = alpha * l_ref[...] + jnp.sum(p, axis=-1, keepdims=True)
        acc = alpha * acc_ref[...]
        for t in range(t_len):
            acc = acc + p[:, t:t + 1] * vn[t:t + 1, :]
        o = acc / l
        rows = [jnp.sum(jnp.where(diag, o[t * N_HEADS:(t + 1) * N_HEADS, :], 0.0), axis=0, keepdims=True)
                for t in range(t_len)]
        o_ref[0] = jnp.concatenate(rows, axis=0)


def _attn_sample(z3, cscol, csn, cache_k4, cache_v4, cache_lft, page_table, layer):
    bsz, t_len, _ = z3.shape
    n_pages = page_table.shape[1]
    pp = min(4, n_pages)
    assert n_pages % pp == 0
    n_rows = t_len * N_HEADS

    def page_spec(shape, pg):
        return pl.BlockSpec(shape, lambda b, g, pt: (layer, pt[b, g * pp + pg], 0, 0))

    z_blk = lambda c: pl.BlockSpec((1, t_len, D_ATTN), lambda b, g, pt: (b, 0, c))
    in_specs = [z_blk(0), z_blk(1), z_blk(2),
                pl.BlockSpec((1, n_rows, 1), lambda b, g, pt: (b, 0, 0)),
                pl.BlockSpec((1, n_rows, t_len), lambda b, g, pt: (b, 0, 0))]
    in_specs += [page_spec((1, 1, PAGE_SIZE, D_ATTN), pg) for pg in range(pp)]
    in_specs += [page_spec((1, 1, PAGE_SIZE, D_ATTN), pg) for pg in range(pp)]
    in_specs += [page_spec((1, 1, N_HEADS, PAGE_SIZE), pg) for pg in range(pp)]
    grid_spec = pltpu.PrefetchScalarGridSpec(
        num_scalar_prefetch=1,
        grid=(bsz, n_pages // pp),
        in_specs=in_specs,
        out_specs=pl.BlockSpec((1, t_len, D_ATTN), lambda b, g, pt: (b, 0, 0)),
        scratch_shapes=[
            pltpu.VMEM((n_rows, D_ATTN), f32),
            pltpu.VMEM((n_rows, D_ATTN), bf16),
            pltpu.VMEM((n_rows, 1), f32),
            pltpu.VMEM((n_rows, 1), f32),
            pltpu.VMEM((n_rows, D_ATTN), f32),
            pltpu.VMEM((N_HEADS, LANE), f32),
        ],
    )
    return pl.pallas_call(
        functools.partial(_decode_kernel, pages_per_step=pp, t_len=t_len),
        grid_spec=grid_spec,
        out_shape=jax.ShapeDtypeStruct((bsz, t_len, D_ATTN), f32),
        compiler_params=_cparams("arbitrary", "arbitrary"),
        name="attn_sample",
    )(page_table, z3, z3, z3, cscol, csn,
      *([cache_k4] * pp), *([cache_v4] * pp), *([cache_lft] * pp))


def _ln_silu_pw(y, lng, lnb, pw16):
    mu = jnp.mean(y, axis=-1, keepdims=True)
    yc = y - mu
    yn = yc * lax.rsqrt(jnp.mean(yc * yc, axis=-1, keepdims=True) + EPS) * lng + lnb
    return jnp.dot(jax.nn.silu(yn).astype(bf16), pw16, preferred_element_type=f32)


CONV_HALO = 32
CONV_CHUNK = 64


def _conv_prompt_kernel(u_ref, w_ref, b_ref, lng_ref, lnb_ref, pw_ref, o_ref, st_ref, ext_ref, y_ref):
    i = pl.program_id(0)
    tr = u_ref.shape[0]
    pad = CONV_HALO - (CONV_WIDTH - 1)

    @pl.when(i == 0)
    def _():
        ext_ref[0:CONV_HALO, :] = jnp.zeros((CONV_HALO, D_CONV), f32)

    ext_ref[CONV_HALO:CONV_HALO + tr, :] = u_ref[...]
    for c in range(tr // CONV_CHUNK):
        r0 = c * CONV_CHUNK
        acc = jnp.broadcast_to(b_ref[...], (CONV_CHUNK, D_CONV))
        for w in range(CONV_WIDTH):
            acc = acc + ext_ref[r0 + pad + w:r0 + pad + w + CONV_CHUNK, :] * w_ref[w:w + 1, :]
        y_ref[r0:r0 + CONV_CHUNK, :] = acc
    o_ref[...] = _ln_silu_pw(y_ref[...], lng_ref[...], lnb_ref[...], pw_ref[...])
    tail = ext_ref[tr:tr + CONV_HALO, :]
    ext_ref[0:CONV_HALO, :] = tail
    st_ref[...] = ext_ref[pad:CONV_HALO, :]


def _conv_prompt(z, dw_w, dw_b, lng, lnb, pw16):
    s_len = z.shape[0]
    tr = min(512, s_len)
    assert s_len % tr == 0 and tr % CONV_CHUNK == 0
    vec = pl.BlockSpec((1, D_CONV), lambda i: (0, 0))
    return pl.pallas_call(
        _conv_prompt_kernel,
        grid=(s_len // tr,),
        in_specs=[
            pl.BlockSpec((tr, D_CONV), lambda i: (i, Z_U)),
            pl.BlockSpec((CONV_WIDTH, D_CONV), lambda i: (0, 0)),
            vec, vec, vec,
            pl.BlockSpec((D_CONV, D_CONV), lambda i: (0, 0)),
        ],
        out_specs=[
            pl.BlockSpec((tr, D_CONV), lambda i: (i, 0)),
            pl.BlockSpec((CONV_WIDTH - 1, D_CONV), lambda i: (0, 0)),
        ],
        out_shape=[
            jax.ShapeDtypeStruct((s_len, D_CONV), f32),
            jax.ShapeDtypeStruct((CONV_WIDTH - 1, D_CONV), f32),
        ],
        scratch_shapes=[
            pltpu.VMEM((tr + CONV_HALO, D_CONV), f32),
            pltpu.VMEM((tr, D_CONV), f32),
        ],
        compiler_params=_cparams("arbitrary"),
        name="conv_prompt",
    )(z, dw_w, dw_b, lng, lnb, pw16)


def _conv_sample_kernel(st_ref, u_ref, w_ref, b_ref, lng_ref, lnb_ref, pw_ref, o_ref, nst_ref):
    n_st = st_ref.shape[0]
    t_len = u_ref.shape[0]
    bsz = u_ref.shape[1]

    def ext(r):
        return st_ref[r] if r < n_st else u_ref[r - n_st]

    for t in range(t_len):
        acc = jnp.broadcast_to(b_ref[...], (bsz, D_CONV))
        for w in range(CONV_WIDTH):
            acc = acc + ext(t + w) * w_ref[w:w + 1, :]
        o_ref[t] = _ln_silu_pw(acc, lng_ref[...], lnb_ref[...], pw_ref[...])
    for r in range(n_st):
        nst_ref[r] = ext(r + t_len)


def _conv_sample(st_t, u_t, dw_w, dw_b, lng, lnb, pw16):
    t_len, bsz, _ = u_t.shape
    return pl.pallas_call(
        _conv_sample_kernel,
        out_shape=[
            jax.ShapeDtypeStruct((t_len, bsz, D_CONV), f32),
            jax.ShapeDtypeStruct(st_t.shape, f32),
        ],
        compiler_params=pltpu.CompilerParams(vmem_limit_bytes=VMEM_LIMIT),
        name="conv_sample",
    )(st_t, u_t, dw_w, dw_b, lng, lnb, pw16)


def _blockdiag(x16, w_ref):
    outs = [jnp.dot(x16[:, n * LRU_BLOCK:(n + 1) * LRU_BLOCK], w_ref[n], preferred_element_type=f32)
            for n in range(LRU_BLOCKS)]
    return jnp.concatenate(outs, axis=1)


def _lru_gates(xc, wa_ref, ba_ref, wx_ref, bx_ref, lam_ref):
    x16 = xc.astype(bf16)
    r = jax.nn.sigmoid(_blockdiag(x16, wa_ref) + ba_ref[...])
    i = jax.nn.sigmoid(_blockdiag(x16, wx_ref) + bx_ref[...])
    log_a = -LRU_C * r * jax.nn.softplus(-lam_ref[...])
    a = jnp.exp(log_a)
    one_minus_a2 = -jnp.tanh(log_a) * (a * a + 1.0)
    b = jnp.sqrt(one_minus_a2) * (i * xc)
    return a, b


LRU_HALO = 8


def _lru_prompt_kernel(x_ref, cw_ref, cb_ref, wa_ref, ba_ref, wx_ref, bx_ref, lam_ref,
                       o_ref, st_ref, hl_ref, ext_ref, a_ref, b_ref, h_ref):
    i = pl.program_id(0)
    tr = x_ref.shape[0]
    pad = LRU_HALO - (LRU_CONV_WIDTH - 1)

    @pl.when(i == 0)
    def _():
        ext_ref[0:LRU_HALO, :] = jnp.zeros((LRU_HALO, D_LRU), f32)
        h_ref[...] = jnp.zeros_like(h_ref)

    ext_ref[LRU_HALO:LRU_HALO + tr, :] = x_ref[...]
    xc = jnp.broadcast_to(cb_ref[...], (tr, D_LRU))
    for w in range(LRU_CONV_WIDTH):
        xc = xc + ext_ref[pad + w:pad + w + tr, :] * cw_ref[w:w + 1, :]
    a, b = _lru_gates(xc, wa_ref, ba_ref, wx_ref, bx_ref, lam_ref)
    a_ref[...] = a
    b_ref[...] = b

    def body(t, h):
        h = a_ref[pl.ds(t, 1), :] * h + b_ref[pl.ds(t, 1), :]
        o_ref[pl.ds(t, 1), :] = h
        return h

    h = lax.fori_loop(0, tr, body, h_ref[...], unroll=8)
    h_ref[...] = h
    hl_ref[...] = h
    tail = ext_ref[tr:tr + LRU_HALO, :]
    ext_ref[0:LRU_HALO, :] = tail
    st_ref[...] = ext_ref[pad:LRU_HALO, :]


def _lru_weight_specs(idx):
    vec = pl.BlockSpec((1, D_LRU), idx)
    blk = pl.BlockSpec((LRU_BLOCKS, LRU_BLOCK, LRU_BLOCK), lambda *a: (0, 0, 0))
    return [pl.BlockSpec((LRU_CONV_WIDTH, D_LRU), idx), vec, blk, vec, blk, vec, vec]


def _lru_prompt(z, lw):
    s_len = z.shape[0]
    tr = min(512, s_len)
    assert s_len % tr == 0
    return pl.pallas_call(
        _lru_prompt_kernel,
        grid=(s_len // tr,),
        in_specs=[pl.BlockSpec((tr, D_LRU), lambda i: (i, Z_LX))] + _lru_weight_specs(lambda i: (0, 0)),
        out_specs=[
            pl.BlockSpec((tr, D_LRU), lambda i: (i, 0)),
            pl.BlockSpec((LRU_CONV_WIDTH - 1, D_LRU), lambda i: (0, 0)),
            pl.BlockSpec((1, D_LRU), lambda i: (0, 0)),
        ],
        out_shape=[
            jax.ShapeDtypeStruct((s_len, D_LRU), f32),
            jax.ShapeDtypeStruct((LRU_CONV_WIDTH - 1, D_LRU), f32),
            jax.ShapeDtypeStruct((1, D_LRU), f32),
        ],
        scratch_shapes=[
            pltpu.VMEM((tr + LRU_HALO, D_LRU), f32),
            pltpu.VMEM((tr, D_LRU), f32),
            pltpu.VMEM((tr, D_LRU), f32),
            pltpu.VMEM((1, D_LRU), f32),
        ],
        compiler_params=_cparams("arbitrary"),
        name="lru_prompt",
    )(z, *lw)


def _lru_sample_kernel(st_ref, x_ref, h0_ref, cw_ref, cb_ref, wa_ref, ba_ref, wx_ref, bx_ref, lam_ref,
                       o_ref, nst_ref, hl_ref):
    n_st = st_ref.shape[0]
    t_len = x_ref.shape[0]
    bsz = x_ref.shape[1]

    def ext(r):
        return st_ref[r] if r < n_st else x_ref[r - n_st]

    h = h0_ref[...]
    for t in range(t_len):
        xc = jnp.broadcast_to(cb_ref[...], (bsz, D_LRU))
        for w in range(LRU_CONV_WIDTH):
            xc = xc + ext(t + w) * cw_ref[w:w + 1, :]
        a, b = _lru_gates(xc, wa_ref, ba_ref, wx_ref, bx_ref, lam_ref)
        h = a * h + b
        o_ref[t] = h
    hl_ref[...] = h
    for r in range(n_st):
        nst_ref[r] = ext(r + t_len)


def _lru_sample(st_t, x_t, h0, lw):
    t_len, bsz, _ = x_t.shape
    return pl.pallas_call(
        _lru_sample_kernel,
        out_shape=[
            jax.ShapeDtypeStruct((t_len, bsz, D_LRU), f32),
            jax.ShapeDtypeStruct(st_t.shape, f32),
            jax.ShapeDtypeStruct((bsz, D_LRU), f32),
        ],
        compiler_params=pltpu.CompilerParams(vmem_limit_bytes=VMEM_LIMIT),
        name="lru_sample",
    )(st_t, x_t, h0, *lw)


MERGE_TN = 1024


def _merge_kernel(oa_ref, oc_ref, ol_ref, ga_ref, gc_ref, gl_ref, x_ref, w_ref, o_ref, y_ref):
    @pl.when(pl.program_id(1) == 0)
    def _():
        y_ref[:, 0:D_ATTN] = (oa_ref[...] * jax.nn.silu(ga_ref[...])).astype(bf16)
        y_ref[:, D_ATTN:D_ATTN + D_CONV] = (oc_ref[...] * jax.nn.silu(gc_ref[...])).astype(bf16)
        y_ref[:, D_ATTN + D_CONV:] = (ol_ref[...] * jax.nn.silu(gl_ref[...])).astype(bf16)

    o_ref[...] = x_ref[...] + jnp.dot(y_ref[...], w_ref[...], preferred_element_type=f32)


def _merge(x2d, o_attn, o_conv, o_lru, z, w_out16):
    n = x2d.shape[0]
    tm = min(512, n)
    assert n % tm == 0
    return pl.pallas_call(
        _merge_kernel,
        grid=(n // tm, D_MODEL // MERGE_TN),
        in_specs=[
            pl.BlockSpec((tm, D_ATTN), lambda i, j: (i, 0)),
            pl.BlockSpec((tm, D_CONV), lambda i, j: (i, 0)),
            pl.BlockSpec((tm, D_LRU), lambda i, j: (i, 0)),
            pl.BlockSpec((tm, D_ATTN), lambda i, j: (i, Z_GA * PROJ_TN // D_ATTN)),
            pl.BlockSpec((tm, D_CONV), lambda i, j: (i, Z_GC)),
            pl.BlockSpec((tm, D_LRU), lambda i, j: (i, Z_GL)),
            pl.BlockSpec((tm, MERGE_TN), lambda i, j: (i, j)),
            pl.BlockSpec((D_MODEL, MERGE_TN), lambda i, j: (0, j)),
        ],
        out_specs=pl.BlockSpec((tm, MERGE_TN), lambda i, j: (i, j)),
        out_shape=jax.ShapeDtypeStruct((n, D_MODEL), f32),
        scratch_shapes=[pltpu.VMEM((tm, D_MODEL), bf16)],
        compiler_params=_cparams("arbitrary", "arbitrary"),
        name="merge",
    )(o_attn, o_conv, o_lru, z, z, z, x2d, w_out16)


def _prep_w_in(w):
    o = 0
    seg = {}
    for name, size in (("q", D_ATTN), ("k", D_ATTN), ("v", D_ATTN), ("f", N_HEADS), ("ga", D_ATTN),
                       ("glu_a", D_CONV), ("glu_b", D_CONV), ("gc", D_CONV), ("lx", D_LRU), ("gl", D_LRU)):
        seg[name] = w[:, o:o + size]
        o += size
    wm = jnp.concatenate([seg[k] for k in ("q", "k", "v", "ga", "gc", "lx", "gl", "glu_b", "glu_a")], axis=1)
    wf = jnp.pad(seg["f"], ((0, 0), (0, LANE - N_HEADS)))
    return wm.astype(bf16), wf.astype(bf16)


def kernel(x_prompt, x_sample, cache_k, cache_v, cache_logf, state_conv, state_lru_conv, state_lru_h,
           page_table, norm_g, w_in, b_f, q_norm_g, k_norm_g, conv_dw_w, conv_dw_b, conv_ln_g, conv_ln_b,
           conv_pw_w, lru_conv_w, lru_conv_b, lru_w_a, lru_b_a, lru_w_x, lru_b_x, lru_lambda, w_out):
    depth = w_in.shape[0]
    n_seq_p, s_len, _ = x_prompt.shape
    bsz, t_len, _ = x_sample.shape
    assert n_seq_p == 1, "the prompt kernels carry one sequence through their row tiles"
    n_pool = cache_k.shape[1]
    cache_k4 = cache_k.reshape(depth, n_pool, PAGE_SIZE, D_ATTN)
    cache_v4 = cache_v.reshape(depth, n_pool, PAGE_SIZE, D_ATTN)
    cache_lft = jnp.swapaxes(cache_logf, 2, 3)

    xp = x_prompt.reshape(s_len, D_MODEL)
    xs = x_sample.reshape(bsz * t_len, D_MODEL)
    outs = {k: [] for k in ("pk", "pv", "pf", "pc", "plc", "ph", "sk", "sv", "sf", "sc", "slc", "sh")}
    row = lambda v: v.reshape(1, -1)
    for l in range(depth):
        wm, wf = _prep_w_in(w_in[l])
        bfp = jnp.pad(b_f[l], (0, LANE - N_HEADS)).reshape(1, LANE)
        proj_w = (row(norm_g[l]), wm, wf, bfp, row(q_norm_g[l]), row(k_norm_g[l]))
        conv_w = (conv_dw_w[l], row(conv_dw_b[l]), row(conv_ln_g[l]), row(conv_ln_b[l]), conv_pw_w[l].astype(bf16))
        lru_w = (lru_conv_w[l], row(lru_conv_b[l]), lru_w_a[l].astype(bf16), row(lru_b_a[l]),
                 lru_w_x[l].astype(bf16), row(lru_b_x[l]), row(lru_lambda[l]))
        w_out16 = w_out[l].astype(bf16)

        z, qkv16, logf, cs = _project(xp, *proj_w, seq_rows=s_len, carry_over=True)
        c_hs = cs[:, :N_HEADS].T
        o_att = _attn_prompt(qkv16, c_hs.reshape(N_HEADS, 1, s_len), c_hs.reshape(N_HEADS, s_len, 1))
        o_conv, st_c = _conv_prompt(z, *conv_w)
        o_lru, st_l, h_l = _lru_prompt(z, lru_w)
        xp = _merge(xp, o_att, o_conv, o_lru, z, w_out16)
        outs["pk"].append(z[:, Z_K * PROJ_TN:Z_K * PROJ_TN + D_ATTN].reshape(1, s_len, N_HEADS, HEAD_DIM))
        outs["pv"].append(z[:, Z_V * PROJ_TN:Z_V * PROJ_TN + D_ATTN].reshape(1, s_len, N_HEADS, HEAD_DIM))
        outs["pf"].append(logf[:, :N_HEADS].reshape(1, s_len, N_HEADS))
        outs["pc"].append(st_c[None])
        outs["plc"].append(st_l[None])
        outs["ph"].append(h_l)

        z, _, logf, cs = _project(xs, *proj_w, seq_rows=t_len, carry_over=False)
        z3 = z.reshape(bsz, t_len, Z_COLS)
        cs3 = cs[:, :N_HEADS].reshape(bsz, t_len, N_HEADS)
        cscol = cs3.reshape(bsz, t_len * N_HEADS, 1)
        csn = jnp.tile(jnp.swapaxes(cs3, 1, 2), (1, t_len, 1))
        o_att = _attn_sample(z3, cscol, csn, cache_k4, cache_v4, cache_lft, page_table, l)
        tmaj = lambda a: jnp.swapaxes(a, 0, 1)
        u_t = tmaj(z3[:, :, Z_U * PROJ_TN:(Z_U + 1) * PROJ_TN])
        o_conv_t, nst_c = _conv_sample(tmaj(state_conv[l]), u_t, *conv_w)
        lx_t = tmaj(z3[:, :, Z_LX * PROJ_TN:(Z_LX + 1) * PROJ_TN])
        o_lru_t, nst_l, h_l = _lru_sample(tmaj(state_lru_conv[l]), lx_t, state_lru_h[l], lru_w)
        xs = _merge(xs, o_att.reshape(bsz * t_len, D_ATTN), tmaj(o_conv_t).reshape(bsz * t_len, D_CONV),
                    tmaj(o_lru_t).reshape(bsz * t_len, D_LRU), z, w_out16)
        outs["sk"].append(z3[:, :, Z_K * PROJ_TN:Z_K * PROJ_TN + D_ATTN].reshape(bsz, t_len, N_HEADS, HEAD_DIM))
        outs["sv"].append(z3[:, :, Z_V * PROJ_TN:Z_V * PROJ_TN + D_ATTN].reshape(bsz, t_len, N_HEADS, HEAD_DIM))
        outs["sf"].append(logf[:, :N_HEADS].reshape(bsz, t_len, N_HEADS))
        outs["sc"].append(tmaj(nst_c))
        outs["slc"].append(tmaj(nst_l))
        outs["sh"].append(h_l)

    st = {k: jnp.stack(v) for k, v in outs.items()}
    return (xp.reshape(1, s_len, D_MODEL), xs.reshape(bsz, t_len, D_MODEL),
            st["pk"], st["pv"], st["pf"], st["pc"], st["plc"], st["ph"],
            st["sk"], st["sv"], st["sf"], st["sc"], st["slc"], st["sh"])
```

```python
import functools
import math

import jax
import jax.numpy as jnp
from jax import lax
from jax.experimental import pallas as pl
from jax.experimental.pallas import tpu as pltpu

f32 = jnp.float32
bf16 = jnp.bfloat16

D_MODEL = 2048
HEAD_DIM = 128
N_HEADS = 8
D_ATTN = N_HEADS * HEAD_DIM
D_CONV = 512
CONV_WIDTH = 31
D_LRU = 512
LRU_BLOCKS = 4
LRU_BLOCK = D_LRU // LRU_BLOCKS
LRU_CONV_WIDTH = 4
LRU_C = 8.0
PAGE_SIZE = 128
EPS = 1e-6
SCALE = HEAD_DIM ** -0.5
LOG2E = math.log2(math.e)

LANE = 128
SUBLANE = 8
VMEM_LIMIT = 48 * 1024 * 1024

PROJ_TN = 512
PROJ_TM = 1024
CUMSUM_ROWS = 256
Z_Q, Z_K, Z_V, Z_GA, Z_GC, Z_LX, Z_GL, Z_SIG, Z_U = 0, 2, 4, 6, 8, 9, 10, 11, 12
Z_COLS = 13 * PROJ_TN


def _cparams(*sem):
    return pltpu.CompilerParams(dimension_semantics=sem, vmem_limit_bytes=VMEM_LIMIT)


def _split3(x):
    hi = x.astype(bf16).astype(f32)
    r1 = x - hi
    mid = r1.astype(bf16).astype(f32)
    lo = (r1 - mid).astype(bf16).astype(f32)
    return hi, mid, lo


def _head_rmsnorm(z, g):
    outs = []
    for h in range(PROJ_TN // HEAD_DIM):
        zh = z[:, h * HEAD_DIM:(h + 1) * HEAD_DIM]
        ms = jnp.mean(zh * zh, axis=-1, keepdims=True)
        outs.append(zh * lax.rsqrt(ms + EPS) * g)
    return jnp.concatenate(outs, axis=1)


def _proj_kernel(x_ref, g_ref, wm_ref, wf_ref, bf_ref, qg_ref, kg_ref,
                 z_ref, qkv_ref, logf_ref, cs_ref,
                 xn_ref, sig_ref, carry_ref, *, seq_rows, carry_over):
    i = pl.program_id(0)
    j = pl.program_id(1)
    tm = x_ref.shape[0]

    @pl.when(j == 0)
    def _():
        tb = min(tm, CUMSUM_ROWS)
        ii = lax.broadcasted_iota(jnp.int32, (tb, tb), 0)
        jj = lax.broadcasted_iota(jnp.int32, (tb, tb), 1)
        keep = jj <= ii
        if not carry_over:
            keep = jnp.logical_and(keep, (ii // seq_rows) == (jj // seq_rows))
        tri = jnp.where(keep, 1.0, 0.0).astype(bf16)
        if carry_over:
            @pl.when(i == 0)
            def _():
                carry_ref[...] = jnp.zeros_like(carry_ref)
        for r0 in range(0, tm, tb):
            x = x_ref[r0:r0 + tb, :]
            y = x * lax.rsqrt(jnp.mean(x * x, axis=-1, keepdims=True) + EPS) * g_ref[...]
            xn = y.astype(bf16)
            xn_ref[r0:r0 + tb, :] = xn
            f = jnp.dot(xn, wf_ref[...], preferred_element_type=f32) + bf_ref[...]
            logf = jax.nn.log_sigmoid(f)
            logf_ref[r0:r0 + tb, :] = logf
            parts = jnp.concatenate(_split3(logf), axis=1).astype(bf16)
            c3 = jnp.dot(tri, parts, preferred_element_type=f32)
            cs = c3[:, :LANE] + c3[:, LANE:2 * LANE] + c3[:, 2 * LANE:]
            if carry_over:
                cs = cs + carry_ref[...]
                carry_ref[...] = cs[tb - 1:tb, :]
            cs_ref[r0:r0 + tb, :] = cs

    z = jnp.dot(xn_ref[...], wm_ref[...], preferred_element_type=f32)

    @pl.when(j < Z_K)
    def _():
        qn = _head_rmsnorm(z, qg_ref[...])
        z_ref[...] = qn
        qkv_ref[...] = (qn * (SCALE * LOG2E)).astype(bf16)

    @pl.when(jnp.logical_and(j >= Z_K, j < Z_V))
    def _():
        kn = _head_rmsnorm(z, kg_ref[...])
        z_ref[...] = kn
        qkv_ref[...] = kn.astype(bf16)

    @pl.when(jnp.logical_and(j >= Z_V, j < Z_GA))
    def _():
        z_ref[...] = z
        qkv_ref[...] = z.astype(bf16)

    @pl.when(jnp.logical_and(j >= Z_GA, j < Z_SIG))
    def _():
        z_ref[...] = z

    @pl.when(j == Z_SIG)
    def _():
        s = jax.nn.sigmoid(z)
        sig_ref[...] = s
        z_ref[...] = s

    @pl.when(j == Z_U)
    def _():
        z_ref[...] = z * sig_ref[...]


def _project(x2d, norm_g, wm, wf, bfp, qg, kg, *, seq_rows, carry_over):
    n = x2d.shape[0]
    tm = min(PROJ_TM, n)
    tb = min(tm, CUMSUM_ROWS)
    assert n % tm == 0 and tm % tb == 0 and (carry_over or tb % seq_rows == 0)
    grid = (n // tm, Z_COLS // PROJ_TN)
    last_qkv = Z_GA - 1
    kern = functools.partial(_proj_kernel, seq_rows=seq_rows, carry_over=carry_over)
    return pl.pallas_call(
        kern,
        grid=grid,
        in_specs=[
            pl.BlockSpec((tm, D_MODEL), lambda i, j: (i, 0)),
            pl.BlockSpec((1, D_MODEL), lambda i, j: (0, 0)),
            pl.BlockSpec((D_MODEL, PROJ_TN), lambda i, j: (0, j)),
            pl.BlockSpec((D_MODEL, LANE), lambda i, j: (0, 0)),
            pl.BlockSpec((1, LANE), lambda i, j: (0, 0)),
            pl.BlockSpec((1, HEAD_DIM), lambda i, j: (0, 0)),
            pl.BlockSpec((1, HEAD_DIM), lambda i, j: (0, 0)),
        ],
        out_specs=[
            pl.BlockSpec((tm, PROJ_TN), lambda i, j: (i, j)),
            pl.BlockSpec((tm, PROJ_TN), lambda i, j: (i, jnp.minimum(j, last_qkv))),
            pl.BlockSpec((tm, LANE), lambda i, j: (i, 0)),
            pl.BlockSpec((tm, LANE), lambda i, j: (i, 0)),
        ],
        out_shape=[
            jax.ShapeDtypeStruct((n, Z_COLS), f32),
            jax.ShapeDtypeStruct((n, 3 * D_ATTN), bf16),
            jax.ShapeDtypeStruct((n, LANE), f32),
            jax.ShapeDtypeStruct((n, LANE), f32),
        ],
        scratch_shapes=[
            pltpu.VMEM((tm, D_MODEL), bf16),
            pltpu.VMEM((tm, PROJ_TN), f32),
            pltpu.VMEM((1, LANE), f32),
        ],
        compiler_params=_cparams("arbitrary", "arbitrary"),
        name="proj",
    )(x2d, norm_g, wm, wf, bfp, qg, kg)


def _attn_prompt_kernel(q_ref, k_ref, v_ref, crow_ref, ccol_ref, g_ref, o_ref, s_ref, p_ref, m_ref, l_ref, acc_ref):
    qi = pl.program_id(1)
    tq = q_ref.shape[0]
    q = q_ref[...]
    cq = ccol_ref[0] * LOG2E

    def chunk(kj):
        return pl.ds(pl.multiple_of(kj * tq, tq), tq)

    def qk_into_s(kj):
        s_ref[...] = lax.dot_general(q, k_ref[chunk(kj), :], (((1,), (1,)), ((), ())),
                                     preferred_element_type=f32)

    def pv(kj):
        return jnp.dot(p_ref[...], v_ref[chunk(kj), :], preferred_element_type=f32)

    def softmax_of_s(kj, masked):
        s = s_ref[...] - crow_ref[0, :, chunk(kj)] * LOG2E
        if masked:
            row = lax.broadcasted_iota(jnp.int32, (tq, tq), 0)
            col = lax.broadcasted_iota(jnp.int32, (tq, tq), 1)
            s = jnp.where(col <= row, s, -jnp.inf)
        m = m_ref[...]
        m_new = jnp.maximum(m, jnp.max(s, axis=-1, keepdims=True))
        shift = (m_new + cq) - cq
        alpha = jnp.exp2(m - shift)
        p = jnp.exp2(s - shift)
        l_ref[...] = alpha * l_ref[...] + jnp.sum(p, axis=-1, keepdims=True)
        m_ref[...] = shift
        return p.astype(bf16), alpha

    m_ref[...] = jnp.full_like(m_ref, -jnp.inf)
    l_ref[...] = jnp.zeros_like(l_ref)
    acc_ref[...] = jnp.zeros_like(acc_ref)
    p_ref[...] = jnp.zeros_like(p_ref)
    qk_into_s(0)

    def body(j, _):
        pv_prev = pv(jnp.maximum(j - 1, 0))
        p, alpha = softmax_of_s(j, False)
        qk_into_s(j + 1)
        p_ref[...] = p
        acc_ref[...] = alpha * (acc_ref[...] + pv_prev)
        return 0

    lax.fori_loop(0, qi, body, 0)
    pv_prev = pv(jnp.maximum(qi - 1, 0))
    p, alpha = softmax_of_s(qi, True)
    p_ref[...] = p
    acc = alpha * (acc_ref[...] + pv_prev) + pv(qi)
    o_ref[...] = ((acc / l_ref[...]) * jax.nn.silu(g_ref[...])).astype(bf16)


def _attn_prompt(qkv16, crow, ccol, z):
    s_len = qkv16.shape[0]
    tq = min(512, s_len)
    assert s_len % tq == 0
    return pl.pallas_call(
        _attn_prompt_kernel,
        grid=(N_HEADS, s_len // tq),
        scratch_shapes=[
            pltpu.VMEM((tq, tq), f32),
            pltpu.VMEM((tq, tq), bf16),
            pltpu.VMEM((tq, 1), f32),
            pltpu.VMEM((tq, 1), f32),
            pltpu.VMEM((tq, HEAD_DIM), f32),
        ],
        in_specs=[
            pl.BlockSpec((tq, HEAD_DIM), lambda h, i: (i, h)),
            pl.BlockSpec((s_len, HEAD_DIM), lambda h, i: (0, N_HEADS + h)),
            pl.BlockSpec((s_len, HEAD_DIM), lambda h, i: (0, 2 * N_HEADS + h)),
            pl.BlockSpec((1, 1, s_len), lambda h, i: (h, 0, 0)),
            pl.BlockSpec((1, tq, 1), lambda h, i: (h, i, 0)),
            pl.BlockSpec((tq, HEAD_DIM), lambda h, i: (i, Z_GA * PROJ_TN // HEAD_DIM + h)),
        ],
        out_specs=pl.BlockSpec((tq, HEAD_DIM), lambda h, i: (i, h)),
        out_shape=jax.ShapeDtypeStruct((s_len, D_ATTN), bf16),
        compiler_params=_cparams("arbitrary", "arbitrary"),
        name="attn_prompt",
    )(qkv16, qkv16, qkv16, crow, ccol, z)


def _decode_kernel(pt_ref, z_q_ref, z_k_ref, z_v_ref, z_g_ref, cscol_ref, csn_ref, *rest, pages_per_step, t_len):
    pp = pages_per_step
    k_refs = rest[:pp]
    v_refs = rest[pp:2 * pp]
    lf_refs = rest[2 * pp:3 * pp]
    o_ref = rest[3 * pp]
    q8_ref, m_ref, l_ref, acc_ref, carry_ref = rest[3 * pp + 1:]
    g = pl.program_id(1)
    n_rows = N_HEADS * SUBLANE
    hsl = lambda h: slice(h * HEAD_DIM, (h + 1) * HEAD_DIM)
    rsl = lambda h: slice(h * SUBLANE, (h + 1) * SUBLANE)

    @pl.when(g == 0)
    def _():
        q8_ref[...] = jnp.zeros_like(q8_ref)
        q = z_q_ref[0] * SCALE
        for h in range(N_HEADS):
            q8_ref[h * SUBLANE:h * SUBLANE + t_len, :] = q[:, hsl(h)]
        m_ref[...] = jnp.full_like(m_ref, -jnp.inf)
        l_ref[...] = jnp.zeros_like(l_ref)
        acc_ref[...] = jnp.zeros_like(acc_ref)
        carry_ref[...] = jnp.zeros_like(carry_ref)

    ii = lax.broadcasted_iota(jnp.int32, (PAGE_SIZE, PAGE_SIZE), 0)
    jj = lax.broadcasted_iota(jnp.int32, (PAGE_SIZE, PAGE_SIZE), 1)
    upper = jnp.where(ii <= jj, 1.0, 0.0).astype(bf16)
    q16 = q8_ref[...].astype(bf16)

    head_rows = lambda ref, h: ref[0, 0, pl.ds(h, PAGE_SIZE, stride=N_HEADS), :].astype(bf16)
    c_pages = []
    carry = carry_ref[...]
    for pg in range(pp):
        parts = jnp.concatenate(_split3(lf_refs[pg][0, 0].T), axis=0).astype(bf16)
        c3 = jnp.dot(parts, upper, preferred_element_type=f32)
        c_page = (c3[:N_HEADS] + c3[N_HEADS:2 * N_HEADS] + c3[2 * N_HEADS:]) + carry
        carry = jnp.broadcast_to(c_page[:, PAGE_SIZE - 1:PAGE_SIZE], (N_HEADS, LANE))
        c_pages.append(c_page)
    carry_ref[...] = carry
    s = jnp.concatenate(
        [jnp.concatenate(
            [lax.dot_general(q16[rsl(h)], head_rows(k_refs[pg], h), (((1,), (1,)), ((), ())),
                             preferred_element_type=f32) - c_pages[pg][h:h + 1, :]
             for h in range(N_HEADS)], axis=0)
         for pg in range(pp)], axis=1)
    m = m_ref[...]
    m_new = jnp.maximum(m, jnp.max(s, axis=-1, keepdims=True))
    alpha = jnp.exp(m - m_new)
    p = jnp.exp(s - m_new)
    l_ref[...] = alpha * l_ref[...] + jnp.sum(p, axis=-1, keepdims=True)
    p16 = p.astype(bf16)
    pv = []
    for h in range(N_HEADS):
        pv_h = jnp.zeros((SUBLANE, HEAD_DIM), f32)
        for pg in range(pp):
            pv_h = pv_h + jnp.dot(p16[rsl(h), pg * PAGE_SIZE:(pg + 1) * PAGE_SIZE], head_rows(v_refs[pg], h),
                                  preferred_element_type=f32)
        pv.append(pv_h)
    acc_ref[...] = alpha * acc_ref[...] + jnp.concatenate(pv, axis=0)
    m_ref[...] = m_new

    @pl.when(g == pl.num_programs(1) - 1)
    def _():
        c_last = jnp.concatenate(
            [jnp.broadcast_to(carry_ref[h:h + 1, 0:1], (SUBLANE, 1)) for h in range(N_HEADS)], axis=0)
        ct_row = c_last + cscol_ref[0]
        ct_new = c_last + csn_ref[0]
        q8 = q8_ref[...]
        kn = z_k_ref[0]
        vn = z_v_ref[0]
        s_n = jnp.concatenate(
            [jnp.concatenate([jnp.sum(q8[rsl(h)] * kn[t:t + 1, hsl(h)], axis=-1, keepdims=True)
                              for t in range(t_len)], axis=1)
             for h in range(N_HEADS)], axis=0) - ct_new
        t_row = lax.broadcasted_iota(jnp.int32, (n_rows, t_len), 0) % SUBLANE
        t_col = lax.broadcasted_iota(jnp.int32, (n_rows, t_len), 1)
        s_n = jnp.where(t_col <= t_row, s_n, -jnp.inf)
        m = m_ref[...]
        m_new = jnp.maximum(m, jnp.max(s_n, axis=-1, keepdims=True))
        shift = (m_new + ct_row) - ct_row
        alpha = jnp.exp(m - shift)
        p = jnp.exp(s_n - shift)
        l = alpha * l_ref[...] + jnp.sum(p, axis=-1, keepdims=True)
        acc = alpha * acc_ref[...]
        for h in range(N_HEADS):
            acc_h = acc[rsl(h)]
            for t in range(t_len):
                acc_h = acc_h + p[rsl(h), t:t + 1] * vn[t:t + 1, hsl(h)]
            o_ref[0, :, hsl(h)] = (acc_h / l[rsl(h)])[:t_len] * jax.nn.silu(z_g_ref[0, :, hsl(h)])


DECODE_PAGES_PER_STEP = 8


def _attn_sample(z3, cscol, csn, cache_k4, cache_v4, cache_logf, page_table, layer):
    bsz, t_len, _ = z3.shape
    n_pages = page_table.shape[1]
    pp = min(DECODE_PAGES_PER_STEP, n_pages)
    assert n_pages % pp == 0 and t_len <= SUBLANE
    n_rows = N_HEADS * SUBLANE

    def page_spec(shape, pg):
        return pl.BlockSpec(shape, lambda b, g, pt: (layer, pt[b, g * pp + pg], 0, 0))

    z_blk = lambda c: pl.BlockSpec((1, t_len, D_ATTN), lambda b, g, pt: (b, 0, c))
    in_specs = [z_blk(0), z_blk(1), z_blk(2), z_blk(Z_GA * PROJ_TN // D_ATTN),
                pl.BlockSpec((1, n_rows, 1), lambda b, g, pt: (b, 0, 0)),
                pl.BlockSpec((1, n_rows, t_len), lambda b, g, pt: (b, 0, 0))]
    in_specs += [page_spec((1, 1, PAGE_SIZE * N_HEADS, HEAD_DIM), pg) for pg in range(pp)]
    in_specs += [page_spec((1, 1, PAGE_SIZE * N_HEADS, HEAD_DIM), pg) for pg in range(pp)]
    in_specs += [page_spec((1, 1, PAGE_SIZE, N_HEADS), pg) for pg in range(pp)]
    grid_spec = pltpu.PrefetchScalarGridSpec(
        num_scalar_prefetch=1,
        grid=(bsz, n_pages // pp),
        in_specs=in_specs,
        out_specs=pl.BlockSpec((1, t_len, D_ATTN), lambda b, g, pt: (b, 0, 0)),
        scratch_shapes=[
            pltpu.VMEM((n_rows, HEAD_DIM), f32),
            pltpu.VMEM((n_rows, 1), f32),
            pltpu.VMEM((n_rows, 1), f32),
            pltpu.VMEM((n_rows, HEAD_DIM), f32),
            pltpu.VMEM((N_HEADS, LANE), f32),
        ],
    )
    return pl.pallas_call(
        functools.partial(_decode_kernel, pages_per_step=pp, t_len=t_len),
        grid_spec=grid_spec,
        out_shape=jax.ShapeDtypeStruct((bsz, t_len, D_ATTN), f32),
        compiler_params=_cparams("arbitrary", "arbitrary"),
        name="attn_sample",
    )(page_table, z3, z3, z3, z3, cscol, csn,
      *([cache_k4] * pp), *([cache_v4] * pp), *([cache_logf] * pp))


def _ln_silu_pw(y, lng, lnb, pw16):
    mu = jnp.mean(y, axis=-1, keepdims=True)
    yc = y - mu
    yn = yc * lax.rsqrt(jnp.mean(yc * yc, axis=-1, keepdims=True) + EPS) * lng + lnb
    return jnp.dot(jax.nn.silu(yn).astype(bf16), pw16, preferred_element_type=f32)


CONV_HALO = 32
CONV_CHUNK = 64


def _conv_prompt_kernel(u_ref, g_ref, w_ref, b_ref, lng_ref, lnb_ref, pw_ref, o_ref, st_ref, ext_ref, y_ref):
    i = pl.program_id(0)
    tr = u_ref.shape[0]
    pad = CONV_HALO - (CONV_WIDTH - 1)

    @pl.when(i == 0)
    def _():
        ext_ref[0:CONV_HALO, :] = jnp.zeros((CONV_HALO, D_CONV), f32)

    ext_ref[CONV_HALO:CONV_HALO + tr, :] = u_ref[...]
    for c in range(tr // CONV_CHUNK):
        r0 = c * CONV_CHUNK
        acc = jnp.broadcast_to(b_ref[...], (CONV_CHUNK, D_CONV))
        for w in range(CONV_WIDTH):
            acc = acc + ext_ref[r0 + pad + w:r0 + pad + w + CONV_CHUNK, :] * w_ref[w:w + 1, :]
        y_ref[r0:r0 + CONV_CHUNK, :] = acc
    o = _ln_silu_pw(y_ref[...], lng_ref[...], lnb_ref[...], pw_ref[...])
    o_ref[...] = (o * jax.nn.silu(g_ref[...])).astype(bf16)
    tail = ext_ref[tr:tr + CONV_HALO, :]
    ext_ref[0:CONV_HALO, :] = tail
    st_ref[...] = ext_ref[pad:CONV_HALO, :]


def _conv_prompt(z, dw_w, dw_b, lng, lnb, pw16):
    s_len = z.shape[0]
    tr = min(512, s_len)
    assert s_len % tr == 0 and tr % CONV_CHUNK == 0
    vec = pl.BlockSpec((1, D_CONV), lambda i: (0, 0))
    return pl.pallas_call(
        _conv_prompt_kernel,
        grid=(s_len // tr,),
        in_specs=[
            pl.BlockSpec((tr, D_CONV), lambda i: (i, Z_U)),
            pl.BlockSpec((tr, D_CONV), lambda i: (i, Z_GC)),
            pl.BlockSpec((CONV_WIDTH, D_CONV), lambda i: (0, 0)),
            vec, vec, vec,
            pl.BlockSpec((D_CONV, D_CONV), lambda i: (0, 0)),
        ],
        out_specs=[
            pl.BlockSpec((tr, D_CONV), lambda i: (i, 0)),
            pl.BlockSpec((CONV_WIDTH - 1, D_CONV), lambda i: (0, 0)),
        ],
        out_shape=[
            jax.ShapeDtypeStruct((s_len, D_CONV), bf16),
            jax.ShapeDtypeStruct((CONV_WIDTH - 1, D_CONV), f32),
        ],
        scratch_shapes=[
            pltpu.VMEM((tr + CONV_HALO, D_CONV), f32),
            pltpu.VMEM((tr, D_CONV), f32),
        ],
        compiler_params=_cparams("arbitrary"),
        name="conv_prompt",
    )(z, z, dw_w, dw_b, lng, lnb, pw16)


def _conv_sample_kernel(st_ref, u_ref, g_ref, w_ref, b_ref, lng_ref, lnb_ref, pw_ref, o_ref, nst_ref):
    n_st = st_ref.shape[0]
    t_len = u_ref.shape[0]
    bsz = u_ref.shape[1]

    def ext(r):
        return st_ref[r] if r < n_st else u_ref[r - n_st]

    for t in range(t_len):
        acc = jnp.broadcast_to(b_ref[...], (bsz, D_CONV))
        for w in range(CONV_WIDTH):
            acc = acc + ext(t + w) * w_ref[w:w + 1, :]
        o_ref[t] = _ln_silu_pw(acc, lng_ref[...], lnb_ref[...], pw_ref[...]) * jax.nn.silu(g_ref[t])
    for r in range(n_st):
        nst_ref[r] = ext(r + t_len)


def _conv_sample(st_t, u_t, g_t, dw_w, dw_b, lng, lnb, pw16):
    t_len, bsz, _ = u_t.shape
    return pl.pallas_call(
        _conv_sample_kernel,
        out_shape=[
            jax.ShapeDtypeStruct((t_len, bsz, D_CONV), f32),
            jax.ShapeDtypeStruct(st_t.shape, f32),
        ],
        compiler_params=pltpu.CompilerParams(vmem_limit_bytes=VMEM_LIMIT),
        name="conv_sample",
    )(st_t, u_t, g_t, dw_w, dw_b, lng, lnb, pw16)


def _blockdiag(x16, w_ref):
    outs = [jnp.dot(x16[:, n * LRU_BLOCK:(n + 1) * LRU_BLOCK], w_ref[n], preferred_element_type=f32)
            for n in range(LRU_BLOCKS)]
    return jnp.concatenate(outs, axis=1)


def _lru_gates(xc, wa_ref, ba_ref, wx_ref, bx_ref, lam_ref):
    x16 = xc.astype(bf16)
    r = jax.nn.sigmoid(_blockdiag(x16, wa_ref) + ba_ref[...])
    i = jax.nn.sigmoid(_blockdiag(x16, wx_ref) + bx_ref[...])
    log_a = -LRU_C * r * jax.nn.softplus(-lam_ref[...])
    a = jnp.exp(log_a)
    one_minus_a2 = -jnp.tanh(log_a) * (a * a + 1.0)
    b = jnp.sqrt(one_minus_a2) * (i * xc)
    return a, b


LRU_HALO = 8


def _lru_prompt_kernel(x_ref, g_ref, cw_ref, cb_ref, wa_ref, ba_ref, wx_ref, bx_ref, lam_ref,
                       o_ref, st_ref, hl_ref, ext_ref, a_ref, b_ref, hs_ref, h_ref):
    i = pl.program_id(0)
    tr = x_ref.shape[0]
    pad = LRU_HALO - (LRU_CONV_WIDTH - 1)

    @pl.when(i == 0)
    def _():
        ext_ref[0:LRU_HALO, :] = jnp.zeros((LRU_HALO, D_LRU), f32)
        h_ref[...] = jnp.zeros_like(h_ref)

    ext_ref[LRU_HALO:LRU_HALO + tr, :] = x_ref[...]
    xc = jnp.broadcast_to(cb_ref[...], (tr, D_LRU))
    for w in range(LRU_CONV_WIDTH):
        xc = xc + ext_ref[pad + w:pad + w + tr, :] * cw_ref[w:w + 1, :]
    a, b = _lru_gates(xc, wa_ref, ba_ref, wx_ref, bx_ref, lam_ref)
    a_ref[...] = a
    b_ref[...] = b

    def body(t, h):
        h = a_ref[pl.ds(t, 1), :] * h + b_ref[pl.ds(t, 1), :]
        hs_ref[pl.ds(t, 1), :] = h
        return h

    h = lax.fori_loop(0, tr, body, h_ref[...], unroll=8)
    h_ref[...] = h
    hl_ref[...] = h
    o_ref[...] = (hs_ref[...] * jax.nn.silu(g_ref[...])).astype(bf16)
    tail = ext_ref[tr:tr + LRU_HALO, :]
    ext_ref[0:LRU_HALO, :] = tail
    st_ref[...] = ext_ref[pad:LRU_HALO, :]


def _lru_weight_specs(idx):
    vec = pl.BlockSpec((1, D_LRU), idx)
    blk = pl.BlockSpec((LRU_BLOCKS, LRU_BLOCK, LRU_BLOCK), lambda *a: (0, 0, 0))
    return [pl.BlockSpec((LRU_CONV_WIDTH, D_LRU), idx), vec, blk, vec, blk, vec, vec]


def _lru_prompt(z, lw):
    s_len = z.shape[0]
    tr = min(512, s_len)
    assert s_len % tr == 0
    return pl.pallas_call(
        _lru_prompt_kernel,
        grid=(s_len // tr,),
        in_specs=[pl.BlockSpec((tr, D_LRU), lambda i: (i, Z_LX)), pl.BlockSpec((tr, D_LRU), lambda i: (i, Z_GL))]
        + _lru_weight_specs(lambda i: (0, 0)),
        out_specs=[
            pl.BlockSpec((tr, D_LRU), lambda i: (i, 0)),
            pl.BlockSpec((LRU_CONV_WIDTH - 1, D_LRU), lambda i: (0, 0)),
            pl.BlockSpec((1, D_LRU), lambda i: (0, 0)),
        ],
        out_shape=[
            jax.ShapeDtypeStruct((s_len, D_LRU), bf16),
            jax.ShapeDtypeStruct((LRU_CONV_WIDTH - 1, D_LRU), f32),
            jax.ShapeDtypeStruct((1, D_LRU), f32),
        ],
        scratch_shapes=[
            pltpu.VMEM((tr + LRU_HALO, D_LRU), f32),
            pltpu.VMEM((tr, D_LRU), f32),
            pltpu.VMEM((tr, D_LRU), f32),
            pltpu.VMEM((tr, D_LRU), f32),
            pltpu.VMEM((1, D_LRU), f32),
        ],
        compiler_params=_cparams("arbitrary"),
        name="lru_prompt",
    )(z, z, *lw)


def _lru_sample_kernel(st_ref, x_ref, g_ref, h0_ref, cw_ref, cb_ref, wa_ref, ba_ref, wx_ref, bx_ref, lam_ref,
                       o_ref, nst_ref, hl_ref):
    n_st = st_ref.shape[0]
    t_len = x_ref.shape[0]
    bsz = x_ref.shape[1]

    def ext(r):
        return st_ref[r] if r < n_st else x_ref[r - n_st]

    h = h0_ref[...]
    for t in range(t_len):
        xc = jnp.broadcast_to(cb_ref[...], (bsz, D_LRU))
        for w in range(LRU_CONV_WIDTH):
            xc = xc + ext(t + w) * cw_ref[w:w + 1, :]
        a, b = _lru_gates(xc, wa_ref, ba_ref, wx_ref, bx_ref, lam_ref)
        h = a * h + b
        o_ref[t] = h * jax.nn.silu(g_ref[t])
    hl_ref[...] = h
    for r in range(n_st):
        nst_ref[r] = ext(r + t_len)


def _lru_sample(st_t, x_t, g_t, h0, lw):
    t_len, bsz, _ = x_t.shape
    return pl.pallas_call(
        _lru_sample_kernel,
        out_shape=[
            jax.ShapeDtypeStruct((t_len, bsz, D_LRU), f32),
            jax.ShapeDtypeStruct(st_t.shape, f32),
            jax.ShapeDtypeStruct((bsz, D_LRU), f32),
        ],
        compiler_params=pltpu.CompilerParams(vmem_limit_bytes=VMEM_LIMIT),
        name="lru_sample",
    )(st_t, x_t, g_t, h0, *lw)


MERGE_TM = 1024
MERGE_TN = 1024


def _merge_kernel(ya_ref, yc_ref, yl_ref, x_ref, w_ref, o_ref):
    acc = jnp.dot(ya_ref[...], w_ref[0:D_ATTN, :], preferred_element_type=f32)
    acc = acc + jnp.dot(yc_ref[...], w_ref[D_ATTN:D_ATTN + D_CONV, :], preferred_element_type=f32)
    acc = acc + jnp.dot(yl_ref[...], w_ref[D_ATTN + D_CONV:, :], preferred_element_type=f32)
    o_ref[...] = x_ref[...] + acc


def _merge(x2d, y_attn, y_conv, y_lru, w_out16):
    n = x2d.shape[0]
    tm = min(MERGE_TM, n)
    assert n % tm == 0
    return pl.pallas_call(
        _merge_kernel,
        grid=(n // tm, D_MODEL // MERGE_TN),
        in_specs=[
            pl.BlockSpec((tm, D_ATTN), lambda i, j: (i, 0)),
            pl.BlockSpec((tm, D_CONV), lambda i, j: (i, 0)),
            pl.BlockSpec((tm, D_LRU), lambda i, j: (i, 0)),
            pl.BlockSpec((tm, MERGE_TN), lambda i, j: (i, j)),
            pl.BlockSpec((D_MODEL, MERGE_TN), lambda i, j: (0, j)),
        ],
        out_specs=pl.BlockSpec((tm, MERGE_TN), lambda i, j: (i, j)),
        out_shape=jax.ShapeDtypeStruct((n, D_MODEL), f32),
        compiler_params=_cparams("arbitrary", "arbitrary"),
        name="merge",
    )(y_attn, y_conv, y_lru, x2d, w_out16)


def _prep_w_in(w):
    o = 0
    seg = {}
    for name, size in (("q", D_ATTN), ("k", D_ATTN), ("v", D_ATTN), ("f", N_HEADS), ("ga", D_ATTN),
                       ("glu_a", D_CONV), ("glu_b", D_CONV), ("gc", D_CONV), ("lx", D_LRU), ("gl", D_LRU)):
        seg[name] = w[:, o:o + size]
        o += size
    wm = jnp.concatenate([seg[k] for k in ("q", "k", "v", "ga", "gc", "lx", "gl", "glu_b", "glu_a")], axis=1)
    wf = jnp.pad(seg["f"], ((0, 0), (0, LANE - N_HEADS)))
    return wm.astype(bf16), wf.astype(bf16)


def kernel(x_prompt, x_sample, cache_k, cache_v, cache_logf, state_conv, state_lru_conv, state_lru_h,
           page_table, norm_g, w_in, b_f, q_norm_g, k_norm_g, conv_dw_w, conv_dw_b, conv_ln_g, conv_ln_b,
           conv_pw_w, lru_conv_w, lru_conv_b, lru_w_a, lru_b_a, lru_w_x, lru_b_x, lru_lambda, w_out):
    depth = w_in.shape[0]
    n_seq_p, s_len, _ = x_prompt.shape
    bsz, t_len, _ = x_sample.shape
    assert n_seq_p == 1, "the prompt kernels carry one sequence through their row tiles"
    n_pool = cache_k.shape[1]
    cache_k4 = cache_k.reshape(depth, n_pool, PAGE_SIZE * N_HEADS, HEAD_DIM)
    cache_v4 = cache_v.reshape(depth, n_pool, PAGE_SIZE * N_HEADS, HEAD_DIM)

    xp = x_prompt.reshape(s_len, D_MODEL)
    xs = x_sample.reshape(bsz * t_len, D_MODEL)
    outs = {k: [] for k in ("pk", "pv", "pf", "pc", "plc", "ph", "sk", "sv", "sf", "sc", "slc", "sh")}
    row = lambda v: v.reshape(1, -1)
    for l in range(depth):
        wm, wf = _prep_w_in(w_in[l])
        bfp = jnp.pad(b_f[l], (0, LANE - N_HEADS)).reshape(1, LANE)
        proj_w = (row(norm_g[l]), wm, wf, bfp, row(q_norm_g[l]), row(k_norm_g[l]))
        conv_w = (conv_dw_w[l], row(conv_dw_b[l]), row(conv_ln_g[l]), row(conv_ln_b[l]), conv_pw_w[l].astype(bf16))
        lru_w = (lru_conv_w[l], row(lru_conv_b[l]), lru_w_a[l].astype(bf16), row(lru_b_a[l]),
                 lru_w_x[l].astype(bf16), row(lru_b_x[l]), row(lru_lambda[l]))
        w_out16 = w_out[l].astype(bf16)

        z, qkv16, logf, cs = _project(xp, *proj_w, seq_rows=s_len, carry_over=True)
        c_hs = cs[:, :N_HEADS].T
        y_att = _attn_prompt(qkv16, c_hs.reshape(N_HEADS, 1, s_len), c_hs.reshape(N_HEADS, s_len, 1), z)
        y_conv, st_c = _conv_prompt(z, *conv_w)
        y_lru, st_l, h_l = _lru_prompt(z, lru_w)
        xp = _merge(xp, y_att, y_conv, y_lru, w_out16)
        outs["pk"].append(z[:, Z_K * PROJ_TN:Z_K * PROJ_TN + D_ATTN].reshape(1, s_len, N_HEADS, HEAD_DIM))
        outs["pv"].append(z[:, Z_V * PROJ_TN:Z_V * PROJ_TN + D_ATTN].reshape(1, s_len, N_HEADS, HEAD_DIM))
        outs["pf"].append(logf[:, :N_HEADS].reshape(1, s_len, N_HEADS))
        outs["pc"].append(st_c[None])
        outs["plc"].append(st_l[None])
        outs["ph"].append(h_l)

        z, _, logf, cs = _project(xs, *proj_w, seq_rows=t_len, carry_over=False)
        z3 = z.reshape(bsz, t_len, Z_COLS)
        cs3 = cs[:, :N_HEADS].reshape(bsz, t_len, N_HEADS)
        cs_ht = jnp.swapaxes(cs3, 1, 2)
        cscol = jnp.pad(cs_ht, ((0, 0), (0, 0), (0, SUBLANE - t_len))).reshape(bsz, N_HEADS * SUBLANE, 1)
        csn = jnp.repeat(cs_ht, SUBLANE, axis=1)
        y_att = _attn_sample(z3, cscol, csn, cache_k4, cache_v4, cache_logf, page_table, l)
        tmaj = lambda a: jnp.swapaxes(a, 0, 1)
        zcol_t = lambda c: tmaj(z3[:, :, c * PROJ_TN:(c + 1) * PROJ_TN])
        bmaj16 = lambda a_t: tmaj(a_t).reshape(bsz * t_len, -1).astype(bf16)
        y_conv_t, nst_c = _conv_sample(tmaj(state_conv[l]), zcol_t(Z_U), zcol_t(Z_GC), *conv_w)
        y_lru_t, nst_l, h_l = _lru_sample(tmaj(state_lru_conv[l]), zcol_t(Z_LX), zcol_t(Z_GL), state_lru_h[l], lru_w)
        xs = _merge(xs, y_att.reshape(bsz * t_len, D_ATTN).astype(bf16), bmaj16(y_conv_t), bmaj16(y_lru_t), w_out16)
        outs["sk"].append(z3[:, :, Z_K * PROJ_TN:Z_K * PROJ_TN + D_ATTN].reshape(bsz, t_len, N_HEADS, HEAD_DIM))
        outs["sv"].append(z3[:, :, Z_V * PROJ_TN:Z_V * PROJ_TN + D_ATTN].reshape(bsz, t_len, N_HEADS, HEAD_DIM))
        outs["sf"].append(logf[:, :N_HEADS].reshape(bsz, t_len, N_HEADS))
        outs["sc"].append(tmaj(nst_c))
        outs["slc"].append(tmaj(nst_l))
        outs["sh"].append(h_l)

    st = {k: jnp.stack(v) for k, v in outs.items()}
    return (xp.reshape(1, s_len, D_MODEL), xs.reshape(bsz, t_len, D_MODEL),
            st["pk"], st["pv"], st["pf"], st["pc"], st["plc"], st["ph"],
            st["sk"], st["sv"], st["sf"], st["sc"], st["slc"], st["sh"])
```

```python
import functools
import math

import jax
import jax.numpy as jnp
from jax import lax
from jax.experimental import pallas as pl
from jax.experimental.pallas import tpu as pltpu

f32 = jnp.float32
bf16 = jnp.bfloat16

D_MODEL = 2048
HEAD_DIM = 128
N_HEADS = 8
D_ATTN = N_HEADS * HEAD_DIM
D_CONV = 512
CONV_WIDTH = 31
D_LRU = 512
LRU_BLOCKS = 4
LRU_BLOCK = D_LRU // LRU_BLOCKS
LRU_CONV_WIDTH = 4
LRU_C = 8.0
PAGE_SIZE = 128
EPS = 1e-6
SCALE = HEAD_DIM ** -0.5
LOG2E = math.log2(math.e)

LANE = 128
SUBLANE = 8
VMEM_LIMIT = 48 * 1024 * 1024

PROJ_TN = 512
PROJ_TM = 1024
CUMSUM_ROWS = 256
W_Q, W_K, W_V, W_GA, W_GC, W_LX, W_GL, W_SIG, W_U = 0, 2, 4, 6, 8, 9, 10, 11, 12
W_COLS = 13 * PROJ_TN
Z_Q, Z_GA, Z_GC, Z_LX, Z_GL, Z_SIG, Z_U = 0, 2, 4, 5, 6, 7, 8
Z_COLS = 9 * PROJ_TN
Z_SHIFT = W_GA - Z_GA


def _cparams(*sem):
    return pltpu.CompilerParams(dimension_semantics=sem, vmem_limit_bytes=VMEM_LIMIT)


def _split3(x):
    hi = x.astype(bf16).astype(f32)
    r1 = x - hi
    mid = r1.astype(bf16).astype(f32)
    lo = (r1 - mid).astype(bf16).astype(f32)
    return hi, mid, lo


def _head_rmsnorm(z, g):
    outs = []
    for h in range(PROJ_TN // HEAD_DIM):
        zh = z[:, h * HEAD_DIM:(h + 1) * HEAD_DIM]
        ms = jnp.mean(zh * zh, axis=-1, keepdims=True)
        outs.append(zh * lax.rsqrt(ms + EPS) * g)
    return jnp.concatenate(outs, axis=1)


def _proj_kernel(x_ref, g_ref, wm_ref, wf_ref, bf_ref, qg_ref, kg_ref,
                 z_ref, k_ref, v_ref, qkv_ref, logf_ref, cs_ref,
                 xn_ref, sig_ref, carry_ref, *, seq_rows, carry_over):
    i = pl.program_id(0)
    j = pl.program_id(1)
    tm = x_ref.shape[0]

    @pl.when(j == 0)
    def _():
        tb = min(tm, CUMSUM_ROWS)
        ii = lax.broadcasted_iota(jnp.int32, (tb, tb), 0)
        jj = lax.broadcasted_iota(jnp.int32, (tb, tb), 1)
        keep = jj <= ii
        if not carry_over:
            keep = jnp.logical_and(keep, (ii // seq_rows) == (jj // seq_rows))
        tri = jnp.where(keep, 1.0, 0.0).astype(bf16)
        if carry_over:
            @pl.when(i == 0)
            def _():
                carry_ref[...] = jnp.zeros_like(carry_ref)
        for r0 in range(0, tm, tb):
            x = x_ref[r0:r0 + tb, :]
            y = x * lax.rsqrt(jnp.mean(x * x, axis=-1, keepdims=True) + EPS) * g_ref[...]
            xn = y.astype(bf16)
            xn_ref[r0:r0 + tb, :] = xn
            f = jnp.dot(xn, wf_ref[...], preferred_element_type=f32) + bf_ref[...]
            logf = jax.nn.log_sigmoid(f)
            logf_ref[r0:r0 + tb, :] = logf
            parts = jnp.concatenate(_split3(logf), axis=1).astype(bf16)
            c3 = jnp.dot(tri, parts, preferred_element_type=f32)
            cs = c3[:, :LANE] + c3[:, LANE:2 * LANE] + c3[:, 2 * LANE:]
            if carry_over:
                cs = cs + carry_ref[...]
                carry_ref[...] = cs[tb - 1:tb, :]
            cs_ref[r0:r0 + tb, :] = cs

    z = jnp.dot(xn_ref[...], wm_ref[...], preferred_element_type=f32)

    @pl.when(j < W_K)
    def _():
        qn = _head_rmsnorm(z, qg_ref[...])
        z_ref[...] = qn
        qkv_ref[...] = (qn * (SCALE * LOG2E)).astype(bf16)

    @pl.when(jnp.logical_and(j >= W_K, j < W_V))
    def _():
        kn = _head_rmsnorm(z, kg_ref[...])
        k_ref[...] = kn
        qkv_ref[...] = kn.astype(bf16)

    @pl.when(jnp.logical_and(j >= W_V, j < W_GA))
    def _():
        v_ref[...] = z
        qkv_ref[...] = z.astype(bf16)

    @pl.when(jnp.logical_and(j >= W_GA, j < W_SIG))
    def _():
        z_ref[...] = z

    @pl.when(j == W_SIG)
    def _():
        s = jax.nn.sigmoid(z)
        sig_ref[...] = s
        z_ref[...] = s

    @pl.when(j == W_U)
    def _():
        z_ref[...] = z * sig_ref[...]


def _project(x2d, norm_g, wm, wf, bfp, qg, kg, *, seq_rows, carry_over):
    n = x2d.shape[0]
    tm = min(PROJ_TM, n)
    tb = min(tm, CUMSUM_ROWS)
    assert n % tm == 0 and tm % tb == 0 and (carry_over or tb % seq_rows == 0)
    grid = (n // tm, W_COLS // PROJ_TN)
    clamp = lambda j, lo, hi: jnp.minimum(jnp.maximum(j, lo), hi) - lo
    kern = functools.partial(_proj_kernel, seq_rows=seq_rows, carry_over=carry_over)
    return pl.pallas_call(
        kern,
        grid=grid,
        in_specs=[
            pl.BlockSpec((tm, D_MODEL), lambda i, j: (i, 0)),
            pl.BlockSpec((1, D_MODEL), lambda i, j: (0, 0)),
            pl.BlockSpec((D_MODEL, PROJ_TN), lambda i, j: (0, j)),
            pl.BlockSpec((D_MODEL, LANE), lambda i, j: (0, 0)),
            pl.BlockSpec((1, LANE), lambda i, j: (0, 0)),
            pl.BlockSpec((1, HEAD_DIM), lambda i, j: (0, 0)),
            pl.BlockSpec((1, HEAD_DIM), lambda i, j: (0, 0)),
        ],
        out_specs=[
            pl.BlockSpec((tm, PROJ_TN), lambda i, j: (i, jnp.where(j < W_K, j, jnp.maximum(j - Z_SHIFT, Z_GA)))),
            pl.BlockSpec((tm, PROJ_TN), lambda i, j: (i, clamp(j, W_K, W_V - 1))),
            pl.BlockSpec((tm, PROJ_TN), lambda i, j: (i, clamp(j, W_V, W_GA - 1))),
            pl.BlockSpec((tm, PROJ_TN), lambda i, j: (i, clamp(j, W_Q, W_GA - 1))),
            pl.BlockSpec((tm, LANE), lambda i, j: (i, 0)),
            pl.BlockSpec((tm, LANE), lambda i, j: (i, 0)),
        ],
        out_shape=[
            jax.ShapeDtypeStruct((n, Z_COLS), f32),
            jax.ShapeDtypeStruct((n, D_ATTN), f32),
            jax.ShapeDtypeStruct((n, D_ATTN), f32),
            jax.ShapeDtypeStruct((n, 3 * D_ATTN), bf16),
            jax.ShapeDtypeStruct((n, LANE), f32),
            jax.ShapeDtypeStruct((n, LANE), f32),
        ],
        scratch_shapes=[
            pltpu.VMEM((tm, D_MODEL), bf16),
            pltpu.VMEM((tm, PROJ_TN), f32),
            pltpu.VMEM((1, LANE), f32),
        ],
        compiler_params=_cparams("arbitrary", "arbitrary"),
        name="proj",
    )(x2d, norm_g, wm, wf, bfp, qg, kg)


def _attn_prompt_kernel(q_ref, k_ref, v_ref, crow_ref, ccol_ref, g_ref, o_ref, s_ref, p_ref, m_ref, acc_ref):
    qi = pl.program_id(1)
    tq = q_ref.shape[0]
    q = q_ref[...]
    cq = ccol_ref[0] * LOG2E

    def chunk(kj):
        return pl.ds(pl.multiple_of(kj * tq, tq), tq)

    def qk_into_s(kj):
        s_ref[...] = lax.dot_general(q, k_ref[chunk(kj), :], (((1,), (1,)), ((), ())),
                                     preferred_element_type=f32)

    ones = jnp.ones((tq, HEAD_DIM), bf16)

    def pv(kj):
        v_aug = jnp.concatenate([v_ref[chunk(kj), :], ones], axis=1)
        return jnp.dot(p_ref[...], v_aug, preferred_element_type=f32)

    def softmax_of_s(kj, masked):
        s = s_ref[...] - crow_ref[0, :, chunk(kj)] * LOG2E
        if masked:
            row = lax.broadcasted_iota(jnp.int32, (tq, tq), 0)
            col = lax.broadcasted_iota(jnp.int32, (tq, tq), 1)
            s = jnp.where(col <= row, s, -jnp.inf)
        m = m_ref[...]
        m_new = jnp.maximum(m, jnp.max(s, axis=-1, keepdims=True))
        shift = (m_new + cq) - cq
        alpha = jnp.exp2(m - shift)
        p = jnp.exp2(s - shift)
        m_ref[...] = shift
        return p.astype(bf16), alpha

    m_ref[...] = jnp.full_like(m_ref, -jnp.inf)
    acc_ref[...] = jnp.zeros_like(acc_ref)
    p_ref[...] = jnp.zeros_like(p_ref)
    qk_into_s(0)

    def body(j, _):
        pv_prev = pv(jnp.maximum(j - 1, 0))
        p, alpha = softmax_of_s(j, False)
        qk_into_s(j + 1)
        p_ref[...] = p
        acc_ref[...] = alpha * (pv_prev + acc_ref[...])
        return 0

    lax.fori_loop(0, qi, body, 0)
    pv_prev = pv(jnp.maximum(qi - 1, 0))
    p, alpha = softmax_of_s(qi, True)
    p_ref[...] = p
    acc = pv(qi) + alpha * (pv_prev + acc_ref[...])
    o_ref[...] = ((acc[:, :HEAD_DIM] / acc[:, HEAD_DIM:]) * jax.nn.silu(g_ref[...])).astype(bf16)


def _attn_prompt(qkv16, crow, ccol, z):
    s_len = qkv16.shape[0]
    tq = min(512, s_len)
    assert s_len % tq == 0
    return pl.pallas_call(
        _attn_prompt_kernel,
        grid=(N_HEADS, s_len // tq),
        scratch_shapes=[
            pltpu.VMEM((tq, tq), f32),
            pltpu.VMEM((tq, tq), bf16),
            pltpu.VMEM((tq, 1), f32),
            pltpu.VMEM((tq, 2 * HEAD_DIM), f32),
        ],
        in_specs=[
            pl.BlockSpec((tq, HEAD_DIM), lambda h, i: (i, h)),
            pl.BlockSpec((s_len, HEAD_DIM), lambda h, i: (0, N_HEADS + h)),
            pl.BlockSpec((s_len, HEAD_DIM), lambda h, i: (0, 2 * N_HEADS + h)),
            pl.BlockSpec((1, 1, s_len), lambda h, i: (h, 0, 0)),
            pl.BlockSpec((1, tq, 1), lambda h, i: (h, i, 0)),
            pl.BlockSpec((tq, HEAD_DIM), lambda h, i: (i, Z_GA * PROJ_TN // HEAD_DIM + h)),
        ],
        out_specs=pl.BlockSpec((tq, HEAD_DIM), lambda h, i: (i, h)),
        out_shape=jax.ShapeDtypeStruct((s_len, D_ATTN), bf16),
        compiler_params=_cparams("arbitrary", "arbitrary"),
        name="attn_prompt",
    )(qkv16, qkv16, qkv16, crow, ccol, z)


def _decode_kernel(pt_ref, z_q_ref, z_g_ref, kn_ref, vn_ref, cscol_ref, csn_ref, *rest, pages_per_step, t_len):
    pp = pages_per_step
    k_refs = rest[:pp]
    v_refs = rest[pp:2 * pp]
    lf_refs = rest[2 * pp:3 * pp]
    o_ref = rest[3 * pp]
    qp_ref, m_ref, l_ref, acc_ref, carry_ref = rest[3 * pp + 1:]
    g = pl.program_id(1)
    n_rows = N_HEADS * SUBLANE
    n_pairs = N_HEADS // 2
    hsl = lambda h: slice(h * HEAD_DIM, (h + 1) * HEAD_DIM)
    rsl = lambda h: slice(h * SUBLANE, (h + 1) * SUBLANE)
    psl = lambda i: slice(2 * i * SUBLANE, 2 * (i + 1) * SUBLANE)
    half = lambda h: slice((h % 2) * HEAD_DIM, (h % 2 + 1) * HEAD_DIM)

    @pl.when(g == 0)
    def _():
        qp_ref[...] = jnp.zeros_like(qp_ref)
        q = z_q_ref[0] * SCALE
        for h in range(N_HEADS):
            qp_ref[h * SUBLANE:h * SUBLANE + t_len, half(h)] = q[:, hsl(h)]
        m_ref[...] = jnp.full_like(m_ref, -jnp.inf)
        l_ref[...] = jnp.zeros_like(l_ref)
        acc_ref[...] = jnp.zeros_like(acc_ref)
        carry_ref[...] = jnp.zeros_like(carry_ref)

    ii = lax.broadcasted_iota(jnp.int32, (PAGE_SIZE, PAGE_SIZE), 0)
    jj = lax.broadcasted_iota(jnp.int32, (PAGE_SIZE, PAGE_SIZE), 1)
    upper = jnp.where(ii <= jj, 1.0, 0.0).astype(bf16)
    q16 = qp_ref[...].astype(bf16)

    head_rows = lambda ref, h: ref[0, 0, pl.ds(h, PAGE_SIZE, stride=N_HEADS), :].astype(bf16)
    c_pages = []
    carry = carry_ref[...]
    for pg in range(pp):
        parts = jnp.concatenate(_split3(lf_refs[pg][0, 0]), axis=0).astype(bf16)
        c3 = jnp.dot(parts, upper, preferred_element_type=f32)
        c_page = (c3[:N_HEADS] + c3[N_HEADS:2 * N_HEADS] + c3[2 * N_HEADS:]) + carry
        carry = jnp.broadcast_to(c_page[:, PAGE_SIZE - 1:PAGE_SIZE], (N_HEADS, LANE))
        c_pages.append(c_page)
    carry_ref[...] = carry
    pair_rows = lambda ref, i: jnp.concatenate([head_rows(ref, 2 * i), head_rows(ref, 2 * i + 1)], axis=1)
    s = jnp.concatenate(
        [jnp.concatenate(
            [lax.dot_general(q16[psl(i)], pair_rows(k_refs[pg], i), (((1,), (1,)), ((), ())),
                             preferred_element_type=f32)
             for i in range(n_pairs)], axis=0) - jnp.repeat(c_pages[pg], SUBLANE, axis=0)
         for pg in range(pp)], axis=1)
    m = m_ref[...]
    m_new = jnp.maximum(m, jnp.max(s, axis=-1, keepdims=True))
    alpha = jnp.exp(m - m_new)
    p = jnp.exp(s - m_new)
    l_ref[...] = alpha * l_ref[...] + jnp.sum(p, axis=-1, keepdims=True)
    p16 = p.astype(bf16)
    pv = []
    for i in range(n_pairs):
        pv_i = jnp.zeros((2 * SUBLANE, 2 * HEAD_DIM), f32)
        for pg in range(pp):
            pv_i = pv_i + jnp.dot(p16[psl(i), pg * PAGE_SIZE:(pg + 1) * PAGE_SIZE], pair_rows(v_refs[pg], i),
                                  preferred_element_type=f32)
        pv += [pv_i[:SUBLANE, :HEAD_DIM], pv_i[SUBLANE:, HEAD_DIM:]]
    acc_ref[...] = alpha * acc_ref[...] + jnp.concatenate(pv, axis=0)
    m_ref[...] = m_new

    @pl.when(g == pl.num_programs(1) - 1)
    def _():
        c_last = jnp.concatenate(
            [jnp.broadcast_to(carry_ref[h:h + 1, 0:1], (SUBLANE, 1)) for h in range(N_HEADS)], axis=0)
        ct_row = c_last + cscol_ref[0]
        ct_new = c_last + csn_ref[0]
        qp = qp_ref[...]
        kn = kn_ref[0]
        vn = vn_ref[0]
        s_n = jnp.concatenate(
            [jnp.concatenate([jnp.sum(qp[rsl(h), half(h)] * kn[t:t + 1, hsl(h)], axis=-1, keepdims=True)
                              for t in range(t_len)], axis=1)
             for h in range(N_HEADS)], axis=0) - ct_new
        t_row = lax.broadcasted_iota(jnp.int32, (n_rows, t_len), 0) % SUBLANE
        t_col = lax.broadcasted_iota(jnp.int32, (n_rows, t_len), 1)
        s_n = jnp.where(t_col <= t_row, s_n, -jnp.inf)
        m = m_ref[...]
        m_new = jnp.maximum(m, jnp.max(s_n, axis=-1, keepdims=True))
        shift = (m_new + ct_row) - ct_row
        alpha = jnp.exp(m - shift)
        p = jnp.exp(s_n - shift)
        l = alpha * l_ref[...] + jnp.sum(p, axis=-1, keepdims=True)
        acc = alpha * acc_ref[...]
        for h in range(N_HEADS):
            acc_h = acc[rsl(h)]
            for t in range(t_len):
                acc_h = acc_h + p[rsl(h), t:t + 1] * vn[t:t + 1, hsl(h)]
            o_ref[0, :, hsl(h)] = (acc_h / l[rsl(h)])[:t_len] * jax.nn.silu(z_g_ref[0, :, hsl(h)])


DECODE_PAGES_PER_STEP = 8


def _attn_sample(z3, k3, v3, cscol, csn, cache_k4, cache_v4, cache_lft, page_table, layer):
    bsz, t_len, _ = z3.shape
    n_pages = page_table.shape[1]
    pp = min(DECODE_PAGES_PER_STEP, n_pages)
    assert n_pages % pp == 0 and t_len <= SUBLANE
    n_rows = N_HEADS * SUBLANE

    def page_spec(shape, pg):
        return pl.BlockSpec(shape, lambda b, g, pt: (layer, pt[b, g * pp + pg], 0, 0))

    z_blk = lambda c: pl.BlockSpec((1, t_len, D_ATTN), lambda b, g, pt: (b, 0, c))
    in_specs = [z_blk(Z_Q * PROJ_TN // D_ATTN), z_blk(Z_GA * PROJ_TN // D_ATTN), z_blk(0), z_blk(0),
                pl.BlockSpec((1, n_rows, 1), lambda b, g, pt: (b, 0, 0)),
                pl.BlockSpec((1, n_rows, t_len), lambda b, g, pt: (b, 0, 0))]
    in_specs += [page_spec((1, 1, PAGE_SIZE * N_HEADS, HEAD_DIM), pg) for pg in range(pp)]
    in_specs += [page_spec((1, 1, PAGE_SIZE * N_HEADS, HEAD_DIM), pg) for pg in range(pp)]
    in_specs += [page_spec((1, 1, N_HEADS, PAGE_SIZE), pg) for pg in range(pp)]
    grid_spec = pltpu.PrefetchScalarGridSpec(
        num_scalar_prefetch=1,
        grid=(bsz, n_pages // pp),
        in_specs=in_specs,
        out_specs=pl.BlockSpec((1, t_len, D_ATTN), lambda b, g, pt: (b, 0, 0)),
        scratch_shapes=[
            pltpu.VMEM((n_rows, 2 * HEAD_DIM), f32),
            pltpu.VMEM((n_rows, 1), f32),
            pltpu.VMEM((n_rows, 1), f32),
            pltpu.VMEM((n_rows, HEAD_DIM), f32),
            pltpu.VMEM((N_HEADS, LANE), f32),
        ],
    )
    return pl.pallas_call(
        functools.partial(_decode_kernel, pages_per_step=pp, t_len=t_len),
        grid_spec=grid_spec,
        out_shape=jax.ShapeDtypeStruct((bsz, t_len, D_ATTN), f32),
        compiler_params=_cparams("arbitrary", "arbitrary"),
        name="attn_sample",
    )(page_table, z3, z3, k3, v3, cscol, csn,
      *([cache_k4] * pp), *([cache_v4] * pp), *([cache_lft] * pp))


def _ln_silu_pw(y, lng, lnb, pw16):
    mu = jnp.mean(y, axis=-1, keepdims=True)
    yc = y - mu
    yn = yc * lax.rsqrt(jnp.mean(yc * yc, axis=-1, keepdims=True) + EPS) * lng + lnb
    return jnp.dot(jax.nn.silu(yn).astype(bf16), pw16, preferred_element_type=f32)


CONV_HALO = 32
CONV_CHUNK = 64


def _conv_prompt_kernel(u_ref, g_ref, w_ref, b_ref, lng_ref, lnb_ref, pw_ref, o_ref, st_ref, ext_ref, y_ref):
    i = pl.program_id(0)
    tr = u_ref.shape[0]
    pad = CONV_HALO - (CONV_WIDTH - 1)

    @pl.when(i == 0)
    def _():
        ext_ref[0:CONV_HALO, :] = jnp.zeros((CONV_HALO, D_CONV), f32)

    ext_ref[CONV_HALO:CONV_HALO + tr, :] = u_ref[...]
    for c in range(tr // CONV_CHUNK):
        r0 = c * CONV_CHUNK
        acc = jnp.broadcast_to(b_ref[...], (CONV_CHUNK, D_CONV))
        for w in range(CONV_WIDTH):
            acc = acc + ext_ref[r0 + pad + w:r0 + pad + w + CONV_CHUNK, :] * w_ref[w:w + 1, :]
        y_ref[r0:r0 + CONV_CHUNK, :] = acc
    o = _ln_silu_pw(y_ref[...], lng_ref[...], lnb_ref[...], pw_ref[...])
    o_ref[...] = (o * jax.nn.silu(g_ref[...])).astype(bf16)
    tail = ext_ref[tr:tr + CONV_HALO, :]
    ext_ref[0:CONV_HALO, :] = tail
    st_ref[...] = ext_ref[pad:CONV_HALO, :]


def _conv_prompt(z, dw_w, dw_b, lng, lnb, pw16):
    s_len = z.shape[0]
    tr = min(512, s_len)
    assert s_len % tr == 0 and tr % CONV_CHUNK == 0
    vec = pl.BlockSpec((1, D_CONV), lambda i: (0, 0))
    return pl.pallas_call(
        _conv_prompt_kernel,
        grid=(s_len // tr,),
        in_specs=[
            pl.BlockSpec((tr, D_CONV), lambda i: (i, Z_U)),
            pl.BlockSpec((tr, D_CONV), lambda i: (i, Z_GC)),
            pl.BlockSpec((CONV_WIDTH, D_CONV), lambda i: (0, 0)),
            vec, vec, vec,
            pl.BlockSpec((D_CONV, D_CONV), lambda i: (0, 0)),
        ],
        out_specs=[
            pl.BlockSpec((tr, D_CONV), lambda i: (i, 0)),
            pl.BlockSpec((CONV_WIDTH - 1, D_CONV), lambda i: (0, 0)),
        ],
        out_shape=[
            jax.ShapeDtypeStruct((s_len, D_CONV), bf16),
            jax.ShapeDtypeStruct((CONV_WIDTH - 1, D_CONV), f32),
        ],
        scratch_shapes=[
            pltpu.VMEM((tr + CONV_HALO, D_CONV), f32),
            pltpu.VMEM((tr, D_CONV), f32),
        ],
        compiler_params=_cparams("arbitrary"),
        name="conv_prompt",
    )(z, z, dw_w, dw_b, lng, lnb, pw16)


def _conv_sample_kernel(st_ref, u_ref, g_ref, w_ref, b_ref, lng_ref, lnb_ref, pw_ref, o_ref, nst_ref):
    n_st = st_ref.shape[0]
    t_len = u_ref.shape[0]
    bsz = u_ref.shape[1]

    def ext(r):
        return st_ref[r] if r < n_st else u_ref[r - n_st]

    for t in range(t_len):
        acc = jnp.broadcast_to(b_ref[...], (bsz, D_CONV))
        for w in range(CONV_WIDTH):
            acc = acc + ext(t + w) * w_ref[w:w + 1, :]
        o_ref[t] = _ln_silu_pw(acc, lng_ref[...], lnb_ref[...], pw_ref[...]) * jax.nn.silu(g_ref[t])
    for r in range(n_st):
        nst_ref[r] = ext(r + t_len)


def _conv_sample(st_t, u_t, g_t, dw_w, dw_b, lng, lnb, pw16):
    t_len, bsz, _ = u_t.shape
    return pl.pallas_call(
        _conv_sample_kernel,
        out_shape=[
            jax.ShapeDtypeStruct((t_len, bsz, D_CONV), f32),
            jax.ShapeDtypeStruct(st_t.shape, f32),
        ],
        compiler_params=pltpu.CompilerParams(vmem_limit_bytes=VMEM_LIMIT),
        name="conv_sample",
    )(st_t, u_t, g_t, dw_w, dw_b, lng, lnb, pw16)


def _blockdiag(x16, w_ref):
    outs = [jnp.dot(x16[:, n * LRU_BLOCK:(n + 1) * LRU_BLOCK], w_ref[n], preferred_element_type=f32)
            for n in range(LRU_BLOCKS)]
    return jnp.concatenate(outs, axis=1)


def _lru_gates(xc, wa_ref, ba_ref, wx_ref, bx_ref, lam_ref):
    x16 = xc.astype(bf16)
    r = jax.nn.sigmoid(_blockdiag(x16, wa_ref) + ba_ref[...])
    i = jax.nn.sigmoid(_blockdiag(x16, wx_ref) + bx_ref[...])
    log_a = -LRU_C * r * jax.nn.softplus(-lam_ref[...])
    a = jnp.exp(log_a)
    one_minus_a2 = -jnp.tanh(log_a) * (a * a + 1.0)
    b = jnp.sqrt(one_minus_a2) * (i * xc)
    return a, b


LRU_HALO = 8


def _lru_prompt_kernel(x_ref, g_ref, cw_ref, cb_ref, wa_ref, ba_ref, wx_ref, bx_ref, lam_ref,
                       o_ref, st_ref, hl_ref, ext_ref, a_ref, b_ref, hs_ref, h_ref):
    i = pl.program_id(0)
    tr = x_ref.shape[0]
    pad = LRU_HALO - (LRU_CONV_WIDTH - 1)

    @pl.when(i == 0)
    def _():
        ext_ref[0:LRU_HALO, :] = jnp.zeros((LRU_HALO, D_LRU), f32)
        h_ref[...] = jnp.zeros_like(h_ref)

    ext_ref[LRU_HALO:LRU_HALO + tr, :] = x_ref[...]
    xc = jnp.broadcast_to(cb_ref[...], (tr, D_LRU))
    for w in range(LRU_CONV_WIDTH):
        xc = xc + ext_ref[pad + w:pad + w + tr, :] * cw_ref[w:w + 1, :]
    a, b = _lru_gates(xc, wa_ref, ba_ref, wx_ref, bx_ref, lam_ref)
    a_ref[...] = a
    b_ref[...] = b

    def body(t, h):
        h = a_ref[pl.ds(t, 1), :] * h + b_ref[pl.ds(t, 1), :]
        hs_ref[pl.ds(t, 1), :] = h
        return h

    h = lax.fori_loop(0, tr, body, h_ref[...], unroll=8)
    h_ref[...] = h
    hl_ref[...] = h
    o_ref[...] = (hs_ref[...] * jax.nn.silu(g_ref[...])).astype(bf16)
    tail = ext_ref[tr:tr + LRU_HALO, :]
    ext_ref[0:LRU_HALO, :] = tail
    st_ref[...] = ext_ref[pad:LRU_HALO, :]


def _lru_weight_specs(idx):
    vec = pl.BlockSpec((1, D_LRU), idx)
    blk = pl.BlockSpec((LRU_BLOCKS, LRU_BLOCK, LRU_BLOCK), lambda *a: (0, 0, 0))
    return [pl.BlockSpec((LRU_CONV_WIDTH, D_LRU), idx), vec, blk, vec, blk, vec, vec]


def _lru_prompt(z, lw):
    s_len = z.shape[0]
    tr = min(512, s_len)
    assert s_len % tr == 0
    return pl.pallas_call(
        _lru_prompt_kernel,
        grid=(s_len // tr,),
        in_specs=[pl.BlockSpec((tr, D_LRU), lambda i: (i, Z_LX)), pl.BlockSpec((tr, D_LRU), lambda i: (i, Z_GL))]
        + _lru_weight_specs(lambda i: (0, 0)),
        out_specs=[
            pl.BlockSpec((tr, D_LRU), lambda i: (i, 0)),
            pl.BlockSpec((LRU_CONV_WIDTH - 1, D_LRU), lambda i: (0, 0)),
            pl.BlockSpec((1, D_LRU), lambda i: (0, 0)),
        ],
        out_shape=[
            jax.ShapeDtypeStruct((s_len, D_LRU), bf16),
            jax.ShapeDtypeStruct((LRU_CONV_WIDTH - 1, D_LRU), f32),
            jax.ShapeDtypeStruct((1, D_LRU), f32),
        ],
        scratch_shapes=[
            pltpu.VMEM((tr + LRU_HALO, D_LRU), f32),
            pltpu.VMEM((tr, D_LRU), f32),
            pltpu.VMEM((tr, D_LRU), f32),
            pltpu.VMEM((tr, D_LRU), f32),
            pltpu.VMEM((1, D_LRU), f32),
        ],
        compiler_params=_cparams("arbitrary"),
        name="lru_prompt",
    )(z, z, *lw)


def _lru_sample_kernel(st_ref, x_ref, g_ref, h0_ref, cw_ref, cb_ref, wa_ref, ba_ref, wx_ref, bx_ref, lam_ref,
                       o_ref, nst_ref, hl_ref):
    n_st = st_ref.shape[0]
    t_len = x_ref.shape[0]
    bsz = x_ref.shape[1]

    def ext(r):
        return st_ref[r] if r < n_st else x_ref[r - n_st]

    h = h0_ref[...]
    for t in range(t_len):
        xc = jnp.broadcast_to(cb_ref[...], (bsz, D_LRU))
        for w in range(LRU_CONV_WIDTH):
            xc = xc + ext(t + w) * cw_ref[w:w + 1, :]
        a, b = _lru_gates(xc, wa_ref, ba_ref, wx_ref, bx_ref, lam_ref)
        h = a * h + b
        o_ref[t] = h * jax.nn.silu(g_ref[t])
    hl_ref[...] = h
    for r in range(n_st):
        nst_ref[r] = ext(r + t_len)


def _lru_sample(st_t, x_t, g_t, h0, lw):
    t_len, bsz, _ = x_t.shape
    return pl.pallas_call(
        _lru_sample_kernel,
        out_shape=[
            jax.ShapeDtypeStruct((t_len, bsz, D_LRU), f32),
            jax.ShapeDtypeStruct(st_t.shape, f32),
            jax.ShapeDtypeStruct((bsz, D_LRU), f32),
        ],
        compiler_params=pltpu.CompilerParams(vmem_limit_bytes=VMEM_LIMIT),
        name="lru_sample",
    )(st_t, x_t, g_t, h0, *lw)


MERGE_TM = 1024
MERGE_TN = 1024


def _merge_kernel(ya_ref, yc_ref, yl_ref, x_ref, w_ref, o_ref):
    acc = jnp.dot(ya_ref[...], w_ref[0:D_ATTN, :], preferred_element_type=f32)
    acc = acc + jnp.dot(yc_ref[...], w_ref[D_ATTN:D_ATTN + D_CONV, :], preferred_element_type=f32)
    acc = acc + jnp.dot(yl_ref[...], w_ref[D_ATTN + D_CONV:, :], preferred_element_type=f32)
    o_ref[...] = x_ref[...] + acc


def _merge(x2d, y_attn, y_conv, y_lru, w_out16):
    n = x2d.shape[0]
    tm = min(MERGE_TM, n)
    assert n % tm == 0
    return pl.pallas_call(
        _merge_kernel,
        grid=(n // tm, D_MODEL // MERGE_TN),
        in_specs=[
            pl.BlockSpec((tm, D_ATTN), lambda i, j: (i, 0)),
            pl.BlockSpec((tm, D_CONV), lambda i, j: (i, 0)),
            pl.BlockSpec((tm, D_LRU), lambda i, j: (i, 0)),
            pl.BlockSpec((tm, MERGE_TN), lambda i, j: (i, j)),
            pl.BlockSpec((D_MODEL, MERGE_TN), lambda i, j: (0, j)),
        ],
        out_specs=pl.BlockSpec((tm, MERGE_TN), lambda i, j: (i, j)),
        out_shape=jax.ShapeDtypeStruct((n, D_MODEL), f32),
        compiler_params=_cparams("arbitrary", "arbitrary"),
        name="merge",
    )(y_attn, y_conv, y_lru, x2d, w_out16)


W_IN_SEGMENTS = (("q", D_ATTN), ("k", D_ATTN), ("v", D_ATTN), ("f", N_HEADS), ("ga", D_ATTN),
                 ("glu_a", D_CONV), ("glu_b", D_CONV), ("gc", D_CONV), ("lx", D_LRU), ("gl", D_LRU))
W_MAIN_ORDER = ("q", "k", "v", "ga", "gc", "lx", "gl", "glu_b", "glu_a")
D_IN = sum(size for _, size in W_IN_SEGMENTS)
PREP_TR = 256


def _prep_w_in_kernel(w_ref, wm_ref, wf_ref):
    w = w_ref[0]
    src = {}
    o = 0
    for name, size in W_IN_SEGMENTS:
        src[name] = (o, size)
        o += size
    d = 0
    for name in W_MAIN_ORDER:
        o, size = src[name]
        wm_ref[:, d:d + size] = w[:, o:o + size].astype(bf16)
        d += size
    o, size = src["f"]
    wf_ref[...] = jnp.zeros_like(wf_ref)
    wf_ref[:, 0:size] = w[:, o:o + size].astype(bf16)


def _prep_w_in(w_in, layer):
    return pl.pallas_call(
        _prep_w_in_kernel,
        grid=(D_MODEL // PREP_TR,),
        in_specs=[pl.BlockSpec((1, PREP_TR, D_IN), lambda i: (layer, i, 0))],
        out_specs=[pl.BlockSpec((PREP_TR, W_COLS), lambda i: (i, 0)), pl.BlockSpec((PREP_TR, LANE), lambda i: (i, 0))],
        out_shape=[jax.ShapeDtypeStruct((D_MODEL, W_COLS), bf16), jax.ShapeDtypeStruct((D_MODEL, LANE), bf16)],
        compiler_params=_cparams("arbitrary"),
        name="prep_w_in",
    )(w_in)


def kernel(x_prompt, x_sample, cache_k, cache_v, cache_logf, state_conv, state_lru_conv, state_lru_h,
           page_table, norm_g, w_in, b_f, q_norm_g, k_norm_g, conv_dw_w, conv_dw_b, conv_ln_g, conv_ln_b,
           conv_pw_w, lru_conv_w, lru_conv_b, lru_w_a, lru_b_a, lru_w_x, lru_b_x, lru_lambda, w_out):
    depth = w_in.shape[0]
    n_seq_p, s_len, _ = x_prompt.shape
    bsz, t_len, _ = x_sample.shape
    assert n_seq_p == 1, "the prompt kernels carry one sequence through their row tiles"
    n_pool = cache_k.shape[1]
    cache_k4 = cache_k.reshape(depth, n_pool, PAGE_SIZE * N_HEADS, HEAD_DIM)
    cache_v4 = cache_v.reshape(depth, n_pool, PAGE_SIZE * N_HEADS, HEAD_DIM)
    cache_lft = jnp.swapaxes(cache_logf, 2, 3)

    xp = x_prompt.reshape(s_len, D_MODEL)
    xs = x_sample.reshape(bsz * t_len, D_MODEL)
    outs = {k: [] for k in ("pk", "pv", "pf", "pc", "plc", "ph", "sk", "sv", "sf", "sc", "slc", "sh")}
    row = lambda v: v.reshape(1, -1)
    for l in range(depth):
        wm, wf = _prep_w_in(w_in, l)
        bfp = jnp.pad(b_f[l], (0, LANE - N_HEADS)).reshape(1, LANE)
        proj_w = (row(norm_g[l]), wm, wf, bfp, row(q_norm_g[l]), row(k_norm_g[l]))
        conv_w = (conv_dw_w[l], row(conv_dw_b[l]), row(conv_ln_g[l]), row(conv_ln_b[l]), conv_pw_w[l].astype(bf16))
        lru_w = (lru_conv_w[l], row(lru_conv_b[l]), lru_w_a[l].astype(bf16), row(lru_b_a[l]),
                 lru_w_x[l].astype(bf16), row(lru_b_x[l]), row(lru_lambda[l]))
        w_out16 = w_out[l].astype(bf16)

        z, k_p, v_p, qkv16, logf, cs = _project(xp, *proj_w, seq_rows=s_len, carry_over=True)
        c_hs = cs[:, :N_HEADS].T
        y_att = _attn_prompt(qkv16, c_hs.reshape(N_HEADS, 1, s_len), c_hs.reshape(N_HEADS, s_len, 1), z)
        y_conv, st_c = _conv_prompt(z, *conv_w)
        y_lru, st_l, h_l = _lru_prompt(z, lru_w)
        xp = _merge(xp, y_att, y_conv, y_lru, w_out16)
        outs["pk"].append(k_p.reshape(1, s_len, N_HEADS, HEAD_DIM))
        outs["pv"].append(v_p.reshape(1, s_len, N_HEADS, HEAD_DIM))
        outs["pf"].append(logf[:, :N_HEADS].reshape(1, s_len, N_HEADS))
        outs["pc"].append(st_c[None])
        outs["plc"].append(st_l[None])
        outs["ph"].append(h_l)

        z, k_s, v_s, _, logf, cs = _project(xs, *proj_w, seq_rows=t_len, carry_over=False)
        z3 = z.reshape(bsz, t_len, Z_COLS)
        k3 = k_s.reshape(bsz, t_len, D_ATTN)
        v3 = v_s.reshape(bsz, t_len, D_ATTN)
        cs3 = cs[:, :N_HEADS].reshape(bsz, t_len, N_HEADS)
        cs_ht = jnp.swapaxes(cs3, 1, 2)
        cscol = jnp.pad(cs_ht, ((0, 0), (0, 0), (0, SUBLANE - t_len))).reshape(bsz, N_HEADS * SUBLANE, 1)
        csn = jnp.repeat(cs_ht, SUBLANE, axis=1)
        y_att = _attn_sample(z3, k3, v3, cscol, csn, cache_k4, cache_v4, cache_lft, page_table, l)
        tmaj = lambda a: jnp.swapaxes(a, 0, 1)
        zcol_t = lambda c: tmaj(z3[:, :, c * PROJ_TN:(c + 1) * PROJ_TN])
        bmaj16 = lambda a_t: tmaj(a_t).reshape(bsz * t_len, -1).astype(bf16)
        y_conv_t, nst_c = _conv_sample(tmaj(state_conv[l]), zcol_t(Z_U), zcol_t(Z_GC), *conv_w)
        y_lru_t, nst_l, h_l = _lru_sample(tmaj(state_lru_conv[l]), zcol_t(Z_LX), zcol_t(Z_GL), state_lru_h[l], lru_w)
        xs = _merge(xs, y_att.reshape(bsz * t_len, D_ATTN).astype(bf16), bmaj16(y_conv_t), bmaj16(y_lru_t), w_out16)
        outs["sk"].append(k3.reshape(bsz, t_len, N_HEADS, HEAD_DIM))
        outs["sv"].append(v3.reshape(bsz, t_len, N_HEADS, HEAD_DIM))
        outs["sf"].append(logf[:, :N_HEADS].reshape(bsz, t_len, N_HEADS))
        outs["sc"].append(tmaj(nst_c))
        outs["slc"].append(tmaj(nst_l))
        outs["sh"].append(h_l)

    st = {k: jnp.stack(v) for k, v in outs.items()}
    return (xp.reshape(1, s_len, D_MODEL), xs.reshape(bsz, t_len, D_MODEL),
            st["pk"], st["pv"], st["pf"], st["pc"], st["plc"], st["ph"],
            st["sk"], st["sv"], st["sf"], st["sc"], st["slc"], st["sh"])
```

```python
import functools
import math

import jax
import jax.numpy as jnp
from jax import lax
from jax.experimental import pallas as pl
from jax.experimental.pallas import tpu as pltpu

f32 = jnp.float32
bf16 = jnp.bfloat16

D_MODEL = 2048
HEAD_DIM = 128
N_HEADS = 8
D_ATTN = N_HEADS * HEAD_DIM
D_CONV = 512
CONV_WIDTH = 31
D_LRU = 512
LRU_BLOCKS = 4
LRU_BLOCK = D_LRU // LRU_BLOCKS
LRU_CONV_WIDTH = 4
LRU_C = 8.0
PAGE_SIZE = 128
EPS = 1e-6
SCALE = HEAD_DIM ** -0.5
LOG2E = math.log2(math.e)

LANE = 128
SUBLANE = 8
VMEM_LIMIT = 48 * 1024 * 1024

PROJ_TN = 512
PROJ_TM = 1024
CUMSUM_ROWS = 256
W_Q, W_K, W_V, W_GA, W_GC, W_LX, W_GL, W_SIG, W_U = 0, 2, 4, 6, 8, 9, 10, 11, 12
W_COLS = 13 * PROJ_TN
Z_Q, Z_GA, Z_GC, Z_LX, Z_GL, Z_SIG, Z_U = 0, 2, 4, 5, 6, 7, 8
Z_COLS = 9 * PROJ_TN
Z_SHIFT = W_GA - Z_GA


def _cparams(*sem):
    return pltpu.CompilerParams(dimension_semantics=sem, vmem_limit_bytes=VMEM_LIMIT)


def _split3(x):
    hi = x.astype(bf16).astype(f32)
    r1 = x - hi
    mid = r1.astype(bf16).astype(f32)
    lo = (r1 - mid).astype(bf16).astype(f32)
    return hi, mid, lo


def _head_rmsnorm(z, g):
    outs = []
    for h in range(PROJ_TN // HEAD_DIM):
        zh = z[:, h * HEAD_DIM:(h + 1) * HEAD_DIM]
        ms = jnp.mean(zh * zh, axis=-1, keepdims=True)
        outs.append(zh * lax.rsqrt(ms + EPS) * g)
    return jnp.concatenate(outs, axis=1)


def _proj_kernel(x_ref, g_ref, wm_ref, wf_ref, bf_ref, qg_ref, kg_ref,
                 z_ref, k_ref, v_ref, qkv_ref, logf_ref, cs_ref,
                 xn_ref, sig_ref, carry_ref, *, seq_rows, carry_over):
    i = pl.program_id(0)
    j = pl.program_id(1)
    tm = x_ref.shape[0]

    @pl.when(j == 0)
    def _():
        tb = min(tm, CUMSUM_ROWS)
        ii = lax.broadcasted_iota(jnp.int32, (tb, tb), 0)
        jj = lax.broadcasted_iota(jnp.int32, (tb, tb), 1)
        keep = jj <= ii
        if not carry_over:
            keep = jnp.logical_and(keep, (ii // seq_rows) == (jj // seq_rows))
        tri = jnp.where(keep, 1.0, 0.0).astype(bf16)
        if carry_over:
            @pl.when(i == 0)
            def _():
                carry_ref[...] = jnp.zeros_like(carry_ref)
        for r0 in range(0, tm, tb):
            x = x_ref[r0:r0 + tb, :]
            y = x * lax.rsqrt(jnp.mean(x * x, axis=-1, keepdims=True) + EPS) * g_ref[...]
            xn = y.astype(bf16)
            xn_ref[r0:r0 + tb, :] = xn
            f = jnp.dot(xn, wf_ref[...], preferred_element_type=f32) + bf_ref[...]
            logf = jax.nn.log_sigmoid(f)
            logf_ref[r0:r0 + tb, :] = logf
            parts = jnp.concatenate(_split3(logf), axis=1).astype(bf16)
            c3 = jnp.dot(tri, parts, preferred_element_type=f32)
            cs = c3[:, :LANE] + c3[:, LANE:2 * LANE] + c3[:, 2 * LANE:]
            if carry_over:
                cs = cs + carry_ref[...]
                carry_ref[...] = cs[tb - 1:tb, :]
            cs_ref[r0:r0 + tb, :] = cs

    z = jnp.dot(xn_ref[...], wm_ref[...], preferred_element_type=f32)

    @pl.when(j < W_K)
    def _():
        qn = _head_rmsnorm(z, qg_ref[...])
        z_ref[...] = qn
        qkv_ref[...] = (qn * (SCALE * LOG2E)).astype(bf16)

    @pl.when(jnp.logical_and(j >= W_K, j < W_V))
    def _():
        kn = _head_rmsnorm(z, kg_ref[...])
        k_ref[...] = kn
        qkv_ref[...] = kn.astype(bf16)

    @pl.when(jnp.logical_and(j >= W_V, j < W_GA))
    def _():
        v_ref[...] = z
        qkv_ref[...] = z.astype(bf16)

    @pl.when(jnp.logical_and(j >= W_GA, j < W_SIG))
    def _():
        z_ref[...] = z

    @pl.when(j == W_SIG)
    def _():
        s = jax.nn.sigmoid(z)
        sig_ref[...] = s
        z_ref[...] = s

    @pl.when(j == W_U)
    def _():
        z_ref[...] = z * sig_ref[...]


def _project(x2d, norm_g, wm, wf, bfp, qg, kg, *, seq_rows, carry_over):
    n = x2d.shape[0]
    tm = min(PROJ_TM, n)
    tb = min(tm, CUMSUM_ROWS)
    assert n % tm == 0 and tm % tb == 0 and (carry_over or tb % seq_rows == 0)
    grid = (n // tm, W_COLS // PROJ_TN)
    clamp = lambda j, lo, hi: jnp.minimum(jnp.maximum(j, lo), hi) - lo
    kern = functools.partial(_proj_kernel, seq_rows=seq_rows, carry_over=carry_over)
    return pl.pallas_call(
        kern,
        grid=grid,
        in_specs=[
            pl.BlockSpec((tm, D_MODEL), lambda i, j: (i, 0)),
            pl.BlockSpec((1, D_MODEL), lambda i, j: (0, 0)),
            pl.BlockSpec((D_MODEL, PROJ_TN), lambda i, j: (0, j)),
            pl.BlockSpec((D_MODEL, LANE), lambda i, j: (0, 0)),
            pl.BlockSpec((1, LANE), lambda i, j: (0, 0)),
            pl.BlockSpec((1, HEAD_DIM), lambda i, j: (0, 0)),
            pl.BlockSpec((1, HEAD_DIM), lambda i, j: (0, 0)),
        ],
        out_specs=[
            pl.BlockSpec((tm, PROJ_TN), lambda i, j: (i, jnp.where(j < W_K, j, jnp.maximum(j - Z_SHIFT, Z_GA)))),
            pl.BlockSpec((tm, PROJ_TN), lambda i, j: (i, clamp(j, W_K, W_V - 1))),
            pl.BlockSpec((tm, PROJ_TN), lambda i, j: (i, clamp(j, W_V, W_GA - 1))),
            pl.BlockSpec((tm, PROJ_TN), lambda i, j: (i, clamp(j, W_Q, W_GA - 1))),
            pl.BlockSpec((tm, LANE), lambda i, j: (i, 0)),
            pl.BlockSpec((tm, LANE), lambda i, j: (i, 0)),
        ],
        out_shape=[
            jax.ShapeDtypeStruct((n, Z_COLS), f32),
            jax.ShapeDtypeStruct((n, D_ATTN), f32),
            jax.ShapeDtypeStruct((n, D_ATTN), f32),
            jax.ShapeDtypeStruct((n, 3 * D_ATTN), bf16),
            jax.ShapeDtypeStruct((n, LANE), f32),
            jax.ShapeDtypeStruct((n, LANE), f32),
        ],
        scratch_shapes=[
            pltpu.VMEM((tm, D_MODEL), bf16),
            pltpu.VMEM((tm, PROJ_TN), f32),
            pltpu.VMEM((1, LANE), f32),
        ],
        compiler_params=_cparams("arbitrary", "arbitrary"),
        name="proj",
    )(x2d, norm_g, wm, wf, bfp, qg, kg)


def _attn_prompt_kernel(q_ref, qn_ref, k_ref, v_ref, crow_ref, cs_ref, g_ref, o_ref, s_ref, p_ref, m_ref, acc_ref):
    qi = pl.program_id(1)
    tq = q_ref.shape[0]
    q = q_ref[...]
    lane = lax.broadcasted_iota(jnp.int32, cs_ref.shape, 1)
    cq = jnp.sum(jnp.where(lane == pl.program_id(0), cs_ref[...], 0.0), axis=-1, keepdims=True) * LOG2E

    def chunk(kj):
        return pl.ds(pl.multiple_of(kj * tq, tq), tq)

    def qk_into_s(kj, q=q):
        s_ref[...] = lax.dot_general(q, k_ref[chunk(kj), :], (((1,), (1,)), ((), ())),
                                     preferred_element_type=f32)

    ones = jnp.ones((tq, HEAD_DIM), bf16)

    def pv(kj):
        v_aug = jnp.concatenate([v_ref[chunk(kj), :], ones], axis=1)
        return jnp.dot(p_ref[...], v_aug, preferred_element_type=f32)

    def softmax_of_s(kj, masked):
        s = s_ref[...] - crow_ref[0, :, chunk(kj)] * LOG2E
        if masked:
            row = lax.broadcasted_iota(jnp.int32, (tq, tq), 0)
            col = lax.broadcasted_iota(jnp.int32, (tq, tq), 1)
            s = jnp.where(col <= row, s, -jnp.inf)
        m = m_ref[...]
        m_new = jnp.maximum(m, jnp.max(s, axis=-1, keepdims=True))
        shift = (m_new + cq) - cq
        alpha = jnp.exp2(m - shift)
        p = jnp.exp2(s - shift)
        m_ref[...] = shift
        return p.astype(bf16), alpha

    m_ref[...] = jnp.full_like(m_ref, -jnp.inf)
    acc_ref[...] = jnp.zeros_like(acc_ref)
    p_ref[...] = jnp.zeros_like(p_ref)

    @pl.when(qi == 0)
    def _():
        qk_into_s(0)

    def body(j, _):
        pv_prev = pv(jnp.maximum(j - 1, 0))
        p, alpha = softmax_of_s(j, False)
        qk_into_s(j + 1)
        p_ref[...] = p
        acc_ref[...] = alpha * (pv_prev + acc_ref[...])
        return 0

    lax.fori_loop(0, qi, body, 0)
    pv_prev = pv(jnp.maximum(qi - 1, 0))
    p, alpha = softmax_of_s(qi, True)
    qk_into_s(0, qn_ref[...])
    p_ref[...] = p
    acc = pv(qi) + alpha * (pv_prev + acc_ref[...])
    o_ref[...] = ((acc[:, :HEAD_DIM] / acc[:, HEAD_DIM:]) * jax.nn.silu(g_ref[...])).astype(bf16)


def _attn_prompt(qkv16, crow, cs, z):
    s_len = qkv16.shape[0]
    tq = min(512, s_len)
    assert s_len % tq == 0
    n_q = s_len // tq
    return pl.pallas_call(
        _attn_prompt_kernel,
        grid=(N_HEADS, n_q),
        scratch_shapes=[
            pltpu.VMEM((tq, tq), f32),
            pltpu.VMEM((tq, tq), bf16),
            pltpu.VMEM((tq, 1), f32),
            pltpu.VMEM((tq, 2 * HEAD_DIM), f32),
        ],
        in_specs=[
            pl.BlockSpec((tq, HEAD_DIM), lambda h, i: (i, h)),
            pl.BlockSpec((tq, HEAD_DIM), lambda h, i: (jnp.minimum(i + 1, n_q - 1), h)),
            pl.BlockSpec((s_len, HEAD_DIM), lambda h, i: (0, N_HEADS + h)),
            pl.BlockSpec((s_len, HEAD_DIM), lambda h, i: (0, 2 * N_HEADS + h)),
            pl.BlockSpec((1, 1, s_len), lambda h, i: (h, 0, 0)),
            pl.BlockSpec((tq, LANE), lambda h, i: (i, 0)),
            pl.BlockSpec((tq, HEAD_DIM), lambda h, i: (i, Z_GA * PROJ_TN // HEAD_DIM + h)),
        ],
        out_specs=pl.BlockSpec((tq, HEAD_DIM), lambda h, i: (i, h)),
        out_shape=jax.ShapeDtypeStruct((s_len, D_ATTN), bf16),
        compiler_params=_cparams("arbitrary", "arbitrary"),
        name="attn_prompt",
    )(qkv16, qkv16, qkv16, qkv16, crow, cs, z)


def _decode_kernel(pt_ref, z_q_ref, z_g_ref, kn_ref, vn_ref, cscol_ref, csn_ref, *rest, pages_per_step, t_len):
    pp = pages_per_step
    k_refs = rest[:pp]
    v_refs = rest[pp:2 * pp]
    lf_refs = rest[2 * pp:3 * pp]
    o_ref = rest[3 * pp]
    qp_ref, m_ref, l_ref, acc_ref, carry_ref = rest[3 * pp + 1:]
    g = pl.program_id(1)
    n_rows = N_HEADS * SUBLANE
    n_pairs = N_HEADS // 2
    hsl = lambda h: slice(h * HEAD_DIM, (h + 1) * HEAD_DIM)
    rsl = lambda h: slice(h * SUBLANE, (h + 1) * SUBLANE)
    psl = lambda i: slice(2 * i * SUBLANE, 2 * (i + 1) * SUBLANE)
    half = lambda h: slice((h % 2) * HEAD_DIM, (h % 2 + 1) * HEAD_DIM)

    @pl.when(g == 0)
    def _():
        qp_ref[...] = jnp.zeros_like(qp_ref)
        q = z_q_ref[0] * SCALE
        for h in range(N_HEADS):
            qp_ref[h * SUBLANE:h * SUBLANE + t_len, half(h)] = q[:, hsl(h)]
        m_ref[...] = jnp.full_like(m_ref, -jnp.inf)
        l_ref[...] = jnp.zeros_like(l_ref)
        acc_ref[...] = jnp.zeros_like(acc_ref)
        carry_ref[...] = jnp.zeros_like(carry_ref)

    ii = lax.broadcasted_iota(jnp.int32, (PAGE_SIZE, PAGE_SIZE), 0)
    jj = lax.broadcasted_iota(jnp.int32, (PAGE_SIZE, PAGE_SIZE), 1)
    upper = jnp.where(ii <= jj, 1.0, 0.0).astype(bf16)
    q16 = qp_ref[...].astype(bf16)

    head_rows = lambda ref, h: ref[0, 0, pl.ds(h, PAGE_SIZE, stride=N_HEADS), :].astype(bf16)
    c_pages = []
    carry = carry_ref[...]
    for pg in range(pp):
        parts = jnp.concatenate(_split3(lf_refs[pg][0, 0]), axis=0).astype(bf16)
        c3 = jnp.dot(parts, upper, preferred_element_type=f32)
        c_page = (c3[:N_HEADS] + c3[N_HEADS:2 * N_HEADS] + c3[2 * N_HEADS:]) + carry
        carry = jnp.broadcast_to(c_page[:, PAGE_SIZE - 1:PAGE_SIZE], (N_HEADS, LANE))
        c_pages.append(c_page)
    carry_ref[...] = carry
    pair_rows = lambda ref, i: jnp.concatenate([head_rows(ref, 2 * i), head_rows(ref, 2 * i + 1)], axis=1)
    s = jnp.concatenate(
        [jnp.concatenate(
            [lax.dot_general(q16[psl(i)], pair_rows(k_refs[pg], i), (((1,), (1,)), ((), ())),
                             preferred_element_type=f32)
             for i in range(n_pairs)], axis=0) - jnp.repeat(c_pages[pg], SUBLANE, axis=0)
         for pg in range(pp)], axis=1)
    m = m_ref[...]
    m_new = jnp.maximum(m, jnp.max(s, axis=-1, keepdims=True))
    alpha = jnp.exp(m - m_new)
    p = jnp.exp(s - m_new)
    l_ref[...] = alpha * l_ref[...] + jnp.sum(p, axis=-1, keepdims=True)
    p16 = p.astype(bf16)
    pv = []
    for i in range(n_pairs):
        pv_i = jnp.zeros((2 * SUBLANE, 2 * HEAD_DIM), f32)
        for pg in range(pp):
            pv_i = pv_i + jnp.dot(p16[psl(i), pg * PAGE_SIZE:(pg + 1) * PAGE_SIZE], pair_rows(v_refs[pg], i),
                                  preferred_element_type=f32)
        pv += [pv_i[:SUBLANE, :HEAD_DIM], pv_i[SUBLANE:, HEAD_DIM:]]
    acc_ref[...] = alpha * acc_ref[...] + jnp.concatenate(pv, axis=0)
    m_ref[...] = m_new

    @pl.when(g == pl.num_programs(1) - 1)
    def _():
        c_last = jnp.concatenate(
            [jnp.broadcast_to(carry_ref[h:h + 1, 0:1], (SUBLANE, 1)) for h in range(N_HEADS)], axis=0)
        ct_row = c_last + cscol_ref[0]
        ct_new = c_last + csn_ref[0]
        qp = qp_ref[...]
        kn = kn_ref[0]
        vn = vn_ref[0]
        s_n = jnp.concatenate(
            [jnp.concatenate([jnp.sum(qp[rsl(h), half(h)] * kn[t:t + 1, hsl(h)], axis=-1, keepdims=True)
                              for t in range(t_len)], axis=1)
             for h in range(N_HEADS)], axis=0) - ct_new
        t_row = lax.broadcasted_iota(jnp.int32, (n_rows, t_len), 0) % SUBLANE
        t_col = lax.broadcasted_iota(jnp.int32, (n_rows, t_len), 1)
        s_n = jnp.where(t_col <= t_row, s_n, -jnp.inf)
        m = m_ref[...]
        m_new = jnp.maximum(m, jnp.max(s_n, axis=-1, keepdims=True))
        shift = (m_new + ct_row) - ct_row
        alpha = jnp.exp(m - shift)
        p = jnp.exp(s_n - shift)
        l = alpha * l_ref[...] + jnp.sum(p, axis=-1, keepdims=True)
        acc = alpha * acc_ref[...]
        for h in range(N_HEADS):
            acc_h = acc[rsl(h)]
            for t in range(t_len):
                acc_h = acc_h + p[rsl(h), t:t + 1] * vn[t:t + 1, hsl(h)]
            o_ref[0, :, hsl(h)] = (acc_h / l[rsl(h)])[:t_len] * jax.nn.silu(z_g_ref[0, :, hsl(h)])


DECODE_PAGES_PER_STEP = 8


def _attn_sample(z3, k3, v3, cscol, csn, cache_k4, cache_v4, cache_lft, page_table, layer):
    bsz, t_len, _ = z3.shape
    n_pages = page_table.shape[1]
    pp = min(DECODE_PAGES_PER_STEP, n_pages)
    assert n_pages % pp == 0 and t_len <= SUBLANE
    n_rows = N_HEADS * SUBLANE

    def page_spec(shape, pg):
        return pl.BlockSpec(shape, lambda b, g, pt: (layer, pt[b, g * pp + pg], 0, 0))

    z_blk = lambda c: pl.BlockSpec((1, t_len, D_ATTN), lambda b, g, pt: (b, 0, c))
    in_specs = [z_blk(Z_Q * PROJ_TN // D_ATTN), z_blk(Z_GA * PROJ_TN // D_ATTN), z_blk(0), z_blk(0),
                pl.BlockSpec((1, n_rows, 1), lambda b, g, pt: (b, 0, 0)),
                pl.BlockSpec((1, n_rows, t_len), lambda b, g, pt: (b, 0, 0))]
    in_specs += [page_spec((1, 1, PAGE_SIZE * N_HEADS, HEAD_DIM), pg) for pg in range(pp)]
    in_specs += [page_spec((1, 1, PAGE_SIZE * N_HEADS, HEAD_DIM), pg) for pg in range(pp)]
    in_specs += [page_spec((1, 1, N_HEADS, PAGE_SIZE), pg) for pg in range(pp)]
    grid_spec = pltpu.PrefetchScalarGridSpec(
        num_scalar_prefetch=1,
        grid=(bsz, n_pages // pp),
        in_specs=in_specs,
        out_specs=pl.BlockSpec((1, t_len, D_ATTN), lambda b, g, pt: (b, 0, 0)),
        scratch_shapes=[
            pltpu.VMEM((n_rows, 2 * HEAD_DIM), f32),
            pltpu.VMEM((n_rows, 1), f32),
            pltpu.VMEM((n_rows, 1), f32),
            pltpu.VMEM((n_rows, HEAD_DIM), f32),
            pltpu.VMEM((N_HEADS, LANE), f32),
        ],
    )
    return pl.pallas_call(
        functools.partial(_decode_kernel, pages_per_step=pp, t_len=t_len),
        grid_spec=grid_spec,
        out_shape=jax.ShapeDtypeStruct((bsz, t_len, D_ATTN), f32),
        compiler_params=_cparams("arbitrary", "arbitrary"),
        name="attn_sample",
    )(page_table, z3, z3, k3, v3, cscol, csn,
      *([cache_k4] * pp), *([cache_v4] * pp), *([cache_lft] * pp))


def _ln_silu_pw(y, lng, lnb, pw16):
    mu = jnp.mean(y, axis=-1, keepdims=True)
    yc = y - mu
    yn = yc * lax.rsqrt(jnp.mean(yc * yc, axis=-1, keepdims=True) + EPS) * lng + lnb
    return jnp.dot(jax.nn.silu(yn).astype(bf16), pw16, preferred_element_type=f32)


CONV_HALO = 32
CONV_CHUNK = 64
CONV_SHIFT_ROWS = CONV_HALO - SUBLANE


def _conv_prompt_kernel(u_ref, g_ref, w_ref, b_ref, lng_ref, lnb_ref, pw_ref, o_ref, st_ref, ext_ref, y_ref, sh_ref):
    i = pl.program_id(0)
    tr = u_ref.shape[0]
    pad = CONV_HALO - (CONV_WIDTH - 1)

    @pl.when(i == 0)
    def _():
        ext_ref[0:CONV_HALO, :] = jnp.zeros((CONV_HALO, D_CONV), f32)

    ext_ref[CONV_HALO:CONV_HALO + tr, :] = u_ref[...]
    for r in range(1, SUBLANE):
        sh_ref[r - 1] = ext_ref[r:r + tr + CONV_SHIFT_ROWS, :]
    for c in range(tr // CONV_CHUNK):
        r0 = c * CONV_CHUNK
        acc = jnp.broadcast_to(b_ref[...], (CONV_CHUNK, D_CONV))
        for w in range(CONV_WIDTH):
            a, r = divmod(pad + w, SUBLANE)
            lo = r0 + a * SUBLANE
            rows = ext_ref[lo:lo + CONV_CHUNK, :] if r == 0 else sh_ref[r - 1, lo:lo + CONV_CHUNK, :]
            acc = acc + rows * w_ref[w:w + 1, :]
        y_ref[r0:r0 + CONV_CHUNK, :] = acc
    o = _ln_silu_pw(y_ref[...], lng_ref[...], lnb_ref[...], pw_ref[...])
    o_ref[...] = (o * jax.nn.silu(g_ref[...])).astype(bf16)
    tail = ext_ref[tr:tr + CONV_HALO, :]
    ext_ref[0:CONV_HALO, :] = tail
    st_ref[...] = ext_ref[pad:CONV_HALO, :]


def _conv_prompt(z, dw_w, dw_b, lng, lnb, pw16):
    s_len = z.shape[0]
    tr = min(512, s_len)
    assert s_len % tr == 0 and tr % CONV_CHUNK == 0
    vec = pl.BlockSpec((1, D_CONV), lambda i: (0, 0))
    return pl.pallas_call(
        _conv_prompt_kernel,
        grid=(s_len // tr,),
        in_specs=[
            pl.BlockSpec((tr, D_CONV), lambda i: (i, Z_U)),
            pl.BlockSpec((tr, D_CONV), lambda i: (i, Z_GC)),
            pl.BlockSpec((CONV_WIDTH, D_CONV), lambda i: (0, 0)),
            vec, vec, vec,
            pl.BlockSpec((D_CONV, D_CONV), lambda i: (0, 0)),
        ],
        out_specs=[
            pl.BlockSpec((tr, D_CONV), lambda i: (i, 0)),
            pl.BlockSpec((CONV_WIDTH - 1, D_CONV), lambda i: (0, 0)),
        ],
        out_shape=[
            jax.ShapeDtypeStruct((s_len, D_CONV), bf16),
            jax.ShapeDtypeStruct((CONV_WIDTH - 1, D_CONV), f32),
        ],
        scratch_shapes=[
            pltpu.VMEM((tr + CONV_HALO, D_CONV), f32),
            pltpu.VMEM((tr, D_CONV), f32),
            pltpu.VMEM((SUBLANE - 1, tr + CONV_SHIFT_ROWS, D_CONV), f32),
        ],
        compiler_params=_cparams("arbitrary"),
        name="conv_prompt",
    )(z, z, dw_w, dw_b, lng, lnb, pw16)


def _conv_sample_kernel(st_ref, u_ref, g_ref, w_ref, b_ref, lng_ref, lnb_ref, pw_ref, o_ref, nst_ref):
    n_st = st_ref.shape[0]
    t_len = u_ref.shape[0]
    bsz = u_ref.shape[1]

    def ext(r):
        return st_ref[r] if r < n_st else u_ref[r - n_st]

    for t in range(t_len):
        acc = jnp.broadcast_to(b_ref[...], (bsz, D_CONV))
        for w in range(CONV_WIDTH):
            acc = acc + ext(t + w) * w_ref[w:w + 1, :]
        o_ref[t] = _ln_silu_pw(acc, lng_ref[...], lnb_ref[...], pw_ref[...]) * jax.nn.silu(g_ref[t])
    for r in range(n_st):
        nst_ref[r] = ext(r + t_len)


def _conv_sample(st_t, u_t, g_t, dw_w, dw_b, lng, lnb, pw16):
    t_len, bsz, _ = u_t.shape
    return pl.pallas_call(
        _conv_sample_kernel,
        out_shape=[
            jax.ShapeDtypeStruct((t_len, bsz, D_CONV), f32),
            jax.ShapeDtypeStruct(st_t.shape, f32),
        ],
        compiler_params=pltpu.CompilerParams(vmem_limit_bytes=VMEM_LIMIT),
        name="conv_sample",
    )(st_t, u_t, g_t, dw_w, dw_b, lng, lnb, pw16)


def _blockdiag(x16, w_ref):
    outs = [jnp.dot(x16[:, n * LRU_BLOCK:(n + 1) * LRU_BLOCK], w_ref[n], preferred_element_type=f32)
            for n in range(LRU_BLOCKS)]
    return jnp.concatenate(outs, axis=1)


def _lru_gates(xc, wa_ref, ba_ref, wx_ref, bx_ref, lam_ref):
    x16 = xc.astype(bf16)
    r = jax.nn.sigmoid(_blockdiag(x16, wa_ref) + ba_ref[...])
    i = jax.nn.sigmoid(_blockdiag(x16, wx_ref) + bx_ref[...])
    log_a = -LRU_C * r * jax.nn.softplus(-lam_ref[...])
    a = jnp.exp(log_a)
    one_minus_a2 = -jnp.tanh(log_a) * (a * a + 1.0)
    b = jnp.sqrt(one_minus_a2) * (i * xc)
    return a, b


LRU_HALO = 8


def _lru_prompt_kernel(x_ref, g_ref, cw_ref, cb_ref, wa_ref, ba_ref, wx_ref, bx_ref, lam_ref,
                       o_ref, st_ref, hl_ref, ext_ref, a_ref, b_ref, hs_ref, h_ref):
    i = pl.program_id(0)
    tr = x_ref.shape[0]
    pad = LRU_HALO - (LRU_CONV_WIDTH - 1)

    @pl.when(i == 0)
    def _():
        ext_ref[0:LRU_HALO, :] = jnp.zeros((LRU_HALO, D_LRU), f32)
        h_ref[...] = jnp.zeros_like(h_ref)

    ext_ref[LRU_HALO:LRU_HALO + tr, :] = x_ref[...]
    xc = jnp.broadcast_to(cb_ref[...], (tr, D_LRU))
    for w in range(LRU_CONV_WIDTH):
        xc = xc + ext_ref[pad + w:pad + w + tr, :] * cw_ref[w:w + 1, :]
    a, b = _lru_gates(xc, wa_ref, ba_ref, wx_ref, bx_ref, lam_ref)
    a_ref[...] = a
    b_ref[...] = b

    def body(t, h):
        h = a_ref[pl.ds(t, 1), :] * h + b_ref[pl.ds(t, 1), :]
        hs_ref[pl.ds(t, 1), :] = h
        return h

    h = lax.fori_loop(0, tr, body, h_ref[...], unroll=8)
    h_ref[...] = h
    hl_ref[...] = h
    o_ref[...] = (hs_ref[...] * jax.nn.silu(g_ref[...])).astype(bf16)
    tail = ext_ref[tr:tr + LRU_HALO, :]
    ext_ref[0:LRU_HALO, :] = tail
    st_ref[...] = ext_ref[pad:LRU_HALO, :]


def _lru_weight_specs(idx):
    vec = pl.BlockSpec((1, D_LRU), idx)
    blk = pl.BlockSpec((LRU_BLOCKS, LRU_BLOCK, LRU_BLOCK), lambda *a: (0, 0, 0))
    return [pl.BlockSpec((LRU_CONV_WIDTH, D_LRU), idx), vec, blk, vec, blk, vec, vec]


def _lru_prompt(z, lw):
    s_len = z.shape[0]
    tr = min(512, s_len)
    assert s_len % tr == 0
    return pl.pallas_call(
        _lru_prompt_kernel,
        grid=(s_len // tr,),
        in_specs=[pl.BlockSpec((tr, D_LRU), lambda i: (i, Z_LX)), pl.BlockSpec((tr, D_LRU), lambda i: (i, Z_GL))]
        + _lru_weight_specs(lambda i: (0, 0)),
        out_specs=[
            pl.BlockSpec((tr, D_LRU), lambda i: (i, 0)),
            pl.BlockSpec((LRU_CONV_WIDTH - 1, D_LRU), lambda i: (0, 0)),
            pl.BlockSpec((1, D_LRU), lambda i: (0, 0)),
        ],
        out_shape=[
            jax.ShapeDtypeStruct((s_len, D_LRU), bf16),
            jax.ShapeDtypeStruct((LRU_CONV_WIDTH - 1, D_LRU), f32),
            jax.ShapeDtypeStruct((1, D_LRU), f32),
        ],
        scratch_shapes=[
            pltpu.VMEM((tr + LRU_HALO, D_LRU), f32),
            pltpu.VMEM((tr, D_LRU), f32),
            pltpu.VMEM((tr, D_LRU), f32),
            pltpu.VMEM((tr, D_LRU), f32),
            pltpu.VMEM((1, D_LRU), f32),
        ],
        compiler_params=_cparams("arbitrary"),
        name="lru_prompt",
    )(z, z, *lw)


def _lru_sample_kernel(st_ref, x_ref, g_ref, h0_ref, cw_ref, cb_ref, wa_ref, ba_ref, wx_ref, bx_ref, lam_ref,
                       o_ref, nst_ref, hl_ref):
    n_st = st_ref.shape[0]
    t_len = x_ref.shape[0]
    bsz = x_ref.shape[1]

    def ext(r):
        return st_ref[r] if r < n_st else x_ref[r - n_st]

    h = h0_ref[...]
    for t in range(t_len):
        xc = jnp.broadcast_to(cb_ref[...], (bsz, D_LRU))
        for w in range(LRU_CONV_WIDTH):
            xc = xc + ext(t + w) * cw_ref[w:w + 1, :]
        a, b = _lru_gates(xc, wa_ref, ba_ref, wx_ref, bx_ref, lam_ref)
        h = a * h + b
        o_ref[t] = h * jax.nn.silu(g_ref[t])
    hl_ref[...] = h
    for r in range(n_st):
        nst_ref[r] = ext(r + t_len)


def _lru_sample(st_t, x_t, g_t, h0, lw):
    t_len, bsz, _ = x_t.shape
    return pl.pallas_call(
        _lru_sample_kernel,
        out_shape=[
            jax.ShapeDtypeStruct((t_len, bsz, D_LRU), f32),
            jax.ShapeDtypeStruct(st_t.shape, f32),
            jax.ShapeDtypeStruct((bsz, D_LRU), f32),
        ],
        compiler_params=pltpu.CompilerParams(vmem_limit_bytes=VMEM_LIMIT),
        name="lru_sample",
    )(st_t, x_t, g_t, h0, *lw)


MERGE_TM = 1024
MERGE_TN = 1024


def _merge_kernel(ya_ref, yc_ref, yl_ref, x_ref, w_ref, o_ref):
    acc = jnp.dot(ya_ref[...], w_ref[0:D_ATTN, :], preferred_element_type=f32)
    acc = acc + jnp.dot(yc_ref[...], w_ref[D_ATTN:D_ATTN + D_CONV, :], preferred_element_type=f32)
    acc = acc + jnp.dot(yl_ref[...], w_ref[D_ATTN + D_CONV:, :], preferred_element_type=f32)
    o_ref[...] = x_ref[...] + acc


def _merge(x2d, y_attn, y_conv, y_lru, w_out16):
    n = x2d.shape[0]
    tm = min(MERGE_TM, n)
    assert n % tm == 0
    return pl.pallas_call(
        _merge_kernel,
        grid=(n // tm, D_MODEL // MERGE_TN),
        in_specs=[
            pl.BlockSpec((tm, D_ATTN), lambda i, j: (i, 0)),
            pl.BlockSpec((tm, D_CONV), lambda i, j: (i, 0)),
            pl.BlockSpec((tm, D_LRU), lambda i, j: (i, 0)),
            pl.BlockSpec((tm, MERGE_TN), lambda i, j: (i, j)),
            pl.BlockSpec((D_MODEL, MERGE_TN), lambda i, j: (0, j)),
        ],
        out_specs=pl.BlockSpec((tm, MERGE_TN), lambda i, j: (i, j)),
        out_shape=jax.ShapeDtypeStruct((n, D_MODEL), f32),
        compiler_params=_cparams("arbitrary", "arbitrary"),
        name="merge",
    )(y_attn, y_conv, y_lru, x2d, w_out16)


W_IN_SEGMENTS = (("q", D_ATTN), ("k", D_ATTN), ("v", D_ATTN), ("f", N_HEADS), ("ga", D_ATTN),
                 ("glu_a", D_CONV), ("glu_b", D_CONV), ("gc", D_CONV), ("lx", D_LRU), ("gl", D_LRU))
W_MAIN_ORDER = ("q", "k", "v", "ga", "gc", "lx", "gl", "glu_b", "glu_a")
D_IN = sum(size for _, size in W_IN_SEGMENTS)
PREP_TR = 256


def _prep_w_in_kernel(wt_ref, wm_ref, wf_ref):
    src = {}
    o = 0
    for name, size in W_IN_SEGMENTS:
        src[name] = (o, size)
        o += size
    d = 0
    for name in W_MAIN_ORDER:
        o, size = src[name]
        wm_ref[:, d:d + size] = wt_ref[0, o:o + size, :].T.astype(bf16)
        d += size
    o, size = src["f"]
    wf_ref[...] = jnp.zeros_like(wf_ref)
    wf_ref[:, 0:size] = wt_ref[0, o:o + size, :].T.astype(bf16)


def _prep_w_in(w_in_t, layer):
    assert all(sum(size for _, size in W_IN_SEGMENTS[:n]) % SUBLANE == 0 for n in range(len(W_IN_SEGMENTS)))
    return pl.pallas_call(
        _prep_w_in_kernel,
        grid=(D_MODEL // PREP_TR,),
        in_specs=[pl.BlockSpec((1, D_IN, PREP_TR), lambda i: (layer, 0, i))],
        out_specs=[pl.BlockSpec((PREP_TR, W_COLS), lambda i: (i, 0)), pl.BlockSpec((PREP_TR, LANE), lambda i: (i, 0))],
        out_shape=[jax.ShapeDtypeStruct((D_MODEL, W_COLS), bf16), jax.ShapeDtypeStruct((D_MODEL, LANE), bf16)],
        compiler_params=_cparams("arbitrary"),
        name="prep_w_in",
    )(w_in_t)


def kernel(x_prompt, x_sample, cache_k, cache_v, cache_logf, state_conv, state_lru_conv, state_lru_h,
           page_table, norm_g, w_in, b_f, q_norm_g, k_norm_g, conv_dw_w, conv_dw_b, conv_ln_g, conv_ln_b,
           conv_pw_w, lru_conv_w, lru_conv_b, lru_w_a, lru_b_a, lru_w_x, lru_b_x, lru_lambda, w_out):
    depth = w_in.shape[0]
    n_seq_p, s_len, _ = x_prompt.shape
    bsz, t_len, _ = x_sample.shape
    assert n_seq_p == 1, "the prompt kernels carry one sequence through their row tiles"
    n_pool = cache_k.shape[1]
    cache_k4 = cache_k.reshape(depth, n_pool, PAGE_SIZE * N_HEADS, HEAD_DIM)
    cache_v4 = cache_v.reshape(depth, n_pool, PAGE_SIZE * N_HEADS, HEAD_DIM)
    cache_lft = jnp.swapaxes(cache_logf, 2, 3)
    w_in_t = jnp.swapaxes(w_in, 1, 2)

    xp = x_prompt.reshape(s_len, D_MODEL)
    xs = x_sample.reshape(bsz * t_len, D_MODEL)
    outs = {k: [] for k in ("pk", "pv", "pf", "pc", "plc", "ph", "sk", "sv", "sf", "sc", "slc", "sh")}
    row = lambda v: v.reshape(1, -1)
    for l in range(depth):
        wm, wf = _prep_w_in(w_in_t, l)
        bfp = jnp.pad(b_f[l], (0, LANE - N_HEADS)).reshape(1, LANE)
        proj_w = (row(norm_g[l]), wm, wf, bfp, row(q_norm_g[l]), row(k_norm_g[l]))
        conv_w = (conv_dw_w[l], row(conv_dw_b[l]), row(conv_ln_g[l]), row(conv_ln_b[l]), conv_pw_w[l].astype(bf16))
        lru_w = (lru_conv_w[l], row(lru_conv_b[l]), lru_w_a[l].astype(bf16), row(lru_b_a[l]),
                 lru_w_x[l].astype(bf16), row(lru_b_x[l]), row(lru_lambda[l]))
        w_out16 = w_out[l].astype(bf16)

        z, k_p, v_p, qkv16, logf, cs = _project(xp, *proj_w, seq_rows=s_len, carry_over=True)
        y_att = _attn_prompt(qkv16, cs[:, :N_HEADS].T.reshape(N_HEADS, 1, s_len), cs, z)
        y_conv, st_c = _conv_prompt(z, *conv_w)
        y_lru, st_l, h_l = _lru_prompt(z, lru_w)
        xp = _merge(xp, y_att, y_conv, y_lru, w_out16)
        outs["pk"].append(k_p.reshape(1, s_len, N_HEADS, HEAD_DIM))
        outs["pv"].append(v_p.reshape(1, s_len, N_HEADS, HEAD_DIM))
        outs["pf"].append(logf[:, :N_HEADS].reshape(1, s_len, N_HEADS))
        outs["pc"].append(st_c[None])
        outs["plc"].append(st_l[None])
        outs["ph"].append(h_l)

        z, k_s, v_s, _, logf, cs = _project(xs, *proj_w, seq_rows=t_len, carry_over=False)
        z3 = z.reshape(bsz, t_len, Z_COLS)
        k3 = k_s.reshape(bsz, t_len, D_ATTN)
        v3 = v_s.reshape(bsz, t_len, D_ATTN)
        cs3 = cs[:, :N_HEADS].reshape(bsz, t_len, N_HEADS)
        cs_ht = jnp.swapaxes(cs3, 1, 2)
        cscol = jnp.pad(cs_ht, ((0, 0), (0, 0), (0, SUBLANE - t_len))).reshape(bsz, N_HEADS * SUBLANE, 1)
        csn = jnp.repeat(cs_ht, SUBLANE, axis=1)
        y_att = _attn_sample(z3, k3, v3, cscol, csn, cache_k4, cache_v4, cache_lft, page_table, l)
        tmaj = lambda a: jnp.swapaxes(a, 0, 1)
        zcol_t = lambda c: tmaj(z3[:, :, c * PROJ_TN:(c + 1) * PROJ_TN])
        bmaj16 = lambda a_t: tmaj(a_t).reshape(bsz * t_len, -1).astype(bf16)
        y_conv_t, nst_c = _conv_sample(tmaj(state_conv[l]), zcol_t(Z_U), zcol_t(Z_GC), *conv_w)
        y_lru_t, nst_l, h_l = _lru_sample(tmaj(state_lru_conv[l]), zcol_t(Z_LX), zcol_t(Z_GL), state_lru_h[l], lru_w)
        xs = _merge(xs, y_att.reshape(bsz * t_len, D_ATTN).astype(bf16), bmaj16(y_conv_t), bmaj16(y_lru_t), w_out16)
        outs["sk"].append(k3.reshape(bsz, t_len, N_HEADS, HEAD_DIM))
        outs["sv"].append(v3.reshape(bsz, t_len, N_HEADS, HEAD_DIM))
        outs["sf"].append(logf[:, :N_HEADS].reshape(bsz, t_len, N_HEADS))
        outs["sc"].append(tmaj(nst_c))
        outs["slc"].append(tmaj(nst_l))
        outs["sh"].append(h_l)

    st = {k: jnp.stack(v) for k, v in outs.items()}
    return (xp.reshape(1, s_len, D_MODEL), xs.reshape(bsz, t_len, D_MODEL),
            st["pk"], st["pv"], st["pf"], st["pc"], st["plc"], st["ph"],
            st["sk"], st["sv"], st["sf"], st["sc"], st["slc"], st["sh"])
```

```python
import functools
import math

import jax
import jax.numpy as jnp
from jax import lax
from jax.experimental import pallas as pl
from jax.experimental.pallas import tpu as pltpu

f32 = jnp.float32
bf16 = jnp.bfloat16

D_MODEL = 2048
HEAD_DIM = 128
N_HEADS = 8
D_ATTN = N_HEADS * HEAD_DIM
D_CONV = 512
CONV_WIDTH = 31
D_LRU = 512
LRU_BLOCKS = 4
LRU_BLOCK = D_LRU // LRU_BLOCKS
LRU_CONV_WIDTH = 4
LRU_C = 8.0
PAGE_SIZE = 128
EPS = 1e-6
SCALE = HEAD_DIM ** -0.5
LOG2E = math.log2(math.e)

LANE = 128
SUBLANE = 8
VMEM_LIMIT = 48 * 1024 * 1024

PROJ_TN = 512
PROJ_TM = 1024
CUMSUM_ROWS = 256
W_Q, W_K, W_V, W_GA, W_GC, W_LX, W_GL, W_SIG, W_U = 0, 2, 4, 6, 8, 9, 10, 11, 12
W_COLS = 13 * PROJ_TN
Z_Q, Z_GA, Z_GC, Z_LX, Z_GL, Z_SIG, Z_U = 0, 2, 4, 5, 6, 7, 8
Z_COLS = 9 * PROJ_TN
Z_SHIFT = W_GA - Z_GA


def _cparams(*sem):
    return pltpu.CompilerParams(dimension_semantics=sem, vmem_limit_bytes=VMEM_LIMIT)


def _split3(x):
    hi = x.astype(bf16).astype(f32)
    r1 = x - hi
    mid = r1.astype(bf16).astype(f32)
    lo = (r1 - mid).astype(bf16).astype(f32)
    return hi, mid, lo


def _head_rmsnorm(z, g):
    outs = []
    for h in range(PROJ_TN // HEAD_DIM):
        zh = z[:, h * HEAD_DIM:(h + 1) * HEAD_DIM]
        ms = jnp.mean(zh * zh, axis=-1, keepdims=True)
        outs.append(zh * lax.rsqrt(ms + EPS) * g)
    return jnp.concatenate(outs, axis=1)


def _proj_kernel(x_ref, g_ref, wm_ref, wf_ref, bf_ref, qg_ref, kg_ref,
                 z_ref, k_ref, v_ref, qkv_ref, logf_ref, cs_ref,
                 xn_ref, sig_ref, carry_ref, *, seq_rows, carry_over):
    i = pl.program_id(0)
    j = pl.program_id(1)
    tm = x_ref.shape[0]

    @pl.when(j == 0)
    def _():
        tb = min(tm, CUMSUM_ROWS)
        ii = lax.broadcasted_iota(jnp.int32, (tb, tb), 0)
        jj = lax.broadcasted_iota(jnp.int32, (tb, tb), 1)
        keep = jj <= ii
        if not carry_over:
            keep = jnp.logical_and(keep, (ii // seq_rows) == (jj // seq_rows))
        tri = jnp.where(keep, 1.0, 0.0).astype(bf16)
        if carry_over:
            @pl.when(i == 0)
            def _():
                carry_ref[...] = jnp.zeros_like(carry_ref)
        for r0 in range(0, tm, tb):
            x = x_ref[r0:r0 + tb, :]
            y = x * lax.rsqrt(jnp.mean(x * x, axis=-1, keepdims=True) + EPS) * g_ref[...]
            xn = y.astype(bf16)
            xn_ref[r0:r0 + tb, :] = xn
            f = jnp.dot(xn, wf_ref[...], preferred_element_type=f32) + bf_ref[...]
            logf = jax.nn.log_sigmoid(f)
            logf_ref[r0:r0 + tb, :] = logf
            parts = jnp.concatenate(_split3(logf), axis=1).astype(bf16)
            c3 = jnp.dot(tri, parts, preferred_element_type=f32)
            cs = c3[:, :LANE] + c3[:, LANE:2 * LANE] + c3[:, 2 * LANE:]
            if carry_over:
                cs = cs + carry_ref[...]
                carry_ref[...] = cs[tb - 1:tb, :]
            cs_ref[r0:r0 + tb, :] = cs

    z = jnp.dot(xn_ref[...], wm_ref[...], preferred_element_type=f32)

    @pl.when(j < W_K)
    def _():
        qn = _head_rmsnorm(z, qg_ref[...])
        z_ref[...] = qn
        qkv_ref[...] = (qn * (SCALE * LOG2E)).astype(bf16)

    @pl.when(jnp.logical_and(j >= W_K, j < W_V))
    def _():
        kn = _head_rmsnorm(z, kg_ref[...])
        k_ref[...] = kn
        qkv_ref[...] = kn.astype(bf16)

    @pl.when(jnp.logical_and(j >= W_V, j < W_GA))
    def _():
        v_ref[...] = z
        qkv_ref[...] = z.astype(bf16)

    @pl.when(jnp.logical_and(j >= W_GA, j < W_SIG))
    def _():
        z_ref[...] = z

    @pl.when(j == W_SIG)
    def _():
        s = jax.nn.sigmoid(z)
        sig_ref[...] = s
        z_ref[...] = s

    @pl.when(j == W_U)
    def _():
        z_ref[...] = z * sig_ref[...]


def _project(x2d, norm_g, wm, wf, bfp, qg, kg, *, seq_rows, carry_over):
    n = x2d.shape[0]
    tm = min(PROJ_TM, n)
    tb = min(tm, CUMSUM_ROWS)
    assert n % tm == 0 and tm % tb == 0 and (carry_over or tb % seq_rows == 0)
    grid = (n // tm, W_COLS // PROJ_TN)
    clamp = lambda j, lo, hi: jnp.minimum(jnp.maximum(j, lo), hi) - lo
    kern = functools.partial(_proj_kernel, seq_rows=seq_rows, carry_over=carry_over)
    return pl.pallas_call(
        kern,
        grid=grid,
        in_specs=[
            pl.BlockSpec((tm, D_MODEL), lambda i, j: (i, 0)),
            pl.BlockSpec((1, D_MODEL), lambda i, j: (0, 0)),
            pl.BlockSpec((D_MODEL, PROJ_TN), lambda i, j: (0, j)),
            pl.BlockSpec((D_MODEL, LANE), lambda i, j: (0, 0)),
            pl.BlockSpec((1, LANE), lambda i, j: (0, 0)),
            pl.BlockSpec((1, HEAD_DIM), lambda i, j: (0, 0)),
            pl.BlockSpec((1, HEAD_DIM), lambda i, j: (0, 0)),
        ],
        out_specs=[
            pl.BlockSpec((tm, PROJ_TN), lambda i, j: (i, jnp.where(j < W_K, j, jnp.maximum(j - Z_SHIFT, Z_GA)))),
            pl.BlockSpec((tm, PROJ_TN), lambda i, j: (i, clamp(j, W_K, W_V - 1))),
            pl.BlockSpec((tm, PROJ_TN), lambda i, j: (i, clamp(j, W_V, W_GA - 1))),
            pl.BlockSpec((tm, PROJ_TN), lambda i, j: (i, clamp(j, W_Q, W_GA - 1))),
            pl.BlockSpec((tm, LANE), lambda i, j: (i, 0)),
            pl.BlockSpec((tm, LANE), lambda i, j: (i, 0)),
        ],
        out_shape=[
            jax.ShapeDtypeStruct((n, Z_COLS), f32),
            jax.ShapeDtypeStruct((n, D_ATTN), f32),
            jax.ShapeDtypeStruct((n, D_ATTN), f32),
            jax.ShapeDtypeStruct((n, 3 * D_ATTN), bf16),
            jax.ShapeDtypeStruct((n, LANE), f32),
            jax.ShapeDtypeStruct((n, LANE), f32),
        ],
        scratch_shapes=[
            pltpu.VMEM((tm, D_MODEL), bf16),
            pltpu.VMEM((tm, PROJ_TN), f32),
            pltpu.VMEM((1, LANE), f32),
        ],
        compiler_params=_cparams("arbitrary", "arbitrary"),
        name="proj",
    )(x2d, norm_g, wm, wf, bfp, qg, kg)


ATTN_TQ = 1024


def _attn_prompt_kernel(q_ref, qn_ref, k_ref, v_ref, crow_ref, cs_ref, g_ref, o_ref, s_ref, p_ref, m_ref, acc_ref):
    qi = pl.program_id(1)
    tq = q_ref.shape[0]
    q = q_ref[...]
    lane = lax.broadcasted_iota(jnp.int32, cs_ref.shape, 1)
    cq = jnp.sum(jnp.where(lane == pl.program_id(0), cs_ref[...], 0.0), axis=-1, keepdims=True) * LOG2E

    def chunk(kj):
        return pl.ds(pl.multiple_of(kj * tq, tq), tq)

    def qk_into_s(kj, q=q):
        s_ref[...] = lax.dot_general(q, k_ref[chunk(kj), :], (((1,), (1,)), ((), ())),
                                     preferred_element_type=f32)

    ones = jnp.ones((tq, HEAD_DIM), bf16)

    def pv(kj):
        v_aug = jnp.concatenate([v_ref[chunk(kj), :], ones], axis=1)
        return jnp.dot(p_ref[...], v_aug, preferred_element_type=f32)

    def softmax_of_s(kj, masked):
        s = s_ref[...] - crow_ref[0, :, chunk(kj)] * LOG2E
        if masked:
            row = lax.broadcasted_iota(jnp.int32, (tq, tq), 0)
            col = lax.broadcasted_iota(jnp.int32, (tq, tq), 1)
            s = jnp.where(col <= row, s, -jnp.inf)
        m = m_ref[...]
        m_new = jnp.maximum(m, jnp.max(s, axis=-1, keepdims=True))
        shift = (m_new + cq) - cq
        alpha = jnp.exp2(m - shift)
        p = jnp.exp2(s - shift)
        m_ref[...] = shift
        return p.astype(bf16), alpha

    m_ref[...] = jnp.full_like(m_ref, -jnp.inf)
    acc_ref[...] = jnp.zeros_like(acc_ref)
    p_ref[...] = jnp.zeros_like(p_ref)

    @pl.when(qi == 0)
    def _():
        qk_into_s(0)

    def body(j, _):
        pv_prev = pv(jnp.maximum(j - 1, 0))
        p, alpha = softmax_of_s(j, False)
        qk_into_s(j + 1)
        p_ref[...] = p
        acc_ref[...] = alpha * (pv_prev + acc_ref[...])
        return 0

    lax.fori_loop(0, qi, body, 0)
    pv_prev = pv(jnp.maximum(qi - 1, 0))
    p, alpha = softmax_of_s(qi, True)
    qk_into_s(0, qn_ref[...])
    p_ref[...] = p
    acc = pv(qi) + alpha * (pv_prev + acc_ref[...])
    o_ref[...] = ((acc[:, :HEAD_DIM] / acc[:, HEAD_DIM:]) * jax.nn.silu(g_ref[...])).astype(bf16)


def _attn_prompt(qkv16, crow, cs, z):
    s_len = qkv16.shape[0]
    tq = min(ATTN_TQ, s_len)
    assert s_len % tq == 0
    n_q = s_len // tq
    return pl.pallas_call(
        _attn_prompt_kernel,
        grid=(N_HEADS, n_q),
        scratch_shapes=[
            pltpu.VMEM((tq, tq), f32),
            pltpu.VMEM((tq, tq), bf16),
            pltpu.VMEM((tq, 1), f32),
            pltpu.VMEM((tq, 2 * HEAD_DIM), f32),
        ],
        in_specs=[
            pl.BlockSpec((tq, HEAD_DIM), lambda h, i: (i, h)),
            pl.BlockSpec((tq, HEAD_DIM), lambda h, i: (jnp.minimum(i + 1, n_q - 1), h)),
            pl.BlockSpec((s_len, HEAD_DIM), lambda h, i: (0, N_HEADS + h)),
            pl.BlockSpec((s_len, HEAD_DIM), lambda h, i: (0, 2 * N_HEADS + h)),
            pl.BlockSpec((1, 1, s_len), lambda h, i: (h, 0, 0)),
            pl.BlockSpec((tq, LANE), lambda h, i: (i, 0)),
            pl.BlockSpec((tq, HEAD_DIM), lambda h, i: (i, Z_GA * PROJ_TN // HEAD_DIM + h)),
        ],
        out_specs=pl.BlockSpec((tq, HEAD_DIM), lambda h, i: (i, h)),
        out_shape=jax.ShapeDtypeStruct((s_len, D_ATTN), bf16),
        compiler_params=_cparams("arbitrary", "arbitrary"),
        name="attn_prompt",
    )(qkv16, qkv16, qkv16, qkv16, crow, cs, z)


def _decode_kernel(pt_ref, z_q_ref, z_g_ref, kn_ref, vn_ref, cscol_ref, csn_ref, *rest, pages_per_step, t_len):
    pp = pages_per_step
    k_refs = rest[:pp]
    v_refs = rest[pp:2 * pp]
    lf_refs = rest[2 * pp:3 * pp]
    o_ref = rest[3 * pp]
    qp_ref, m_ref, l_ref, acc_ref, carry_ref = rest[3 * pp + 1:]
    g = pl.program_id(1)
    n_rows = N_HEADS * SUBLANE
    n_pairs = N_HEADS // 2
    hsl = lambda h: slice(h * HEAD_DIM, (h + 1) * HEAD_DIM)
    rsl = lambda h: slice(h * SUBLANE, (h + 1) * SUBLANE)
    psl = lambda i: slice(2 * i * SUBLANE, 2 * (i + 1) * SUBLANE)
    half = lambda h: slice((h % 2) * HEAD_DIM, (h % 2 + 1) * HEAD_DIM)

    @pl.when(g == 0)
    def _():
        qp_ref[...] = jnp.zeros_like(qp_ref)
        q = z_q_ref[0] * SCALE
        for h in range(N_HEADS):
            qp_ref[h * SUBLANE:h * SUBLANE + t_len, half(h)] = q[:, hsl(h)]
        m_ref[...] = jnp.full_like(m_ref, -jnp.inf)
        l_ref[...] = jnp.zeros_like(l_ref)
        acc_ref[...] = jnp.zeros_like(acc_ref)
        carry_ref[...] = jnp.zeros_like(carry_ref)

    ii = lax.broadcasted_iota(jnp.int32, (PAGE_SIZE, PAGE_SIZE), 0)
    jj = lax.broadcasted_iota(jnp.int32, (PAGE_SIZE, PAGE_SIZE), 1)
    upper = jnp.where(ii <= jj, 1.0, 0.0).astype(bf16)
    q16 = qp_ref[...].astype(bf16)

    head_rows = lambda ref, h: ref[0, 0, pl.ds(h, PAGE_SIZE, stride=N_HEADS), :].astype(bf16)
    c_pages = []
    carry = carry_ref[...]
    for pg in range(pp):
        parts = jnp.concatenate(_split3(lf_refs[pg][0, 0]), axis=0).astype(bf16)
        c3 = jnp.dot(parts, upper, preferred_element_type=f32)
        c_page = (c3[:N_HEADS] + c3[N_HEADS:2 * N_HEADS] + c3[2 * N_HEADS:]) + carry
        carry = jnp.broadcast_to(c_page[:, PAGE_SIZE - 1:PAGE_SIZE], (N_HEADS, LANE))
        c_pages.append(c_page)
    carry_ref[...] = carry
    pair_rows = lambda ref, i: jnp.concatenate([head_rows(ref, 2 * i), head_rows(ref, 2 * i + 1)], axis=1)
    s = jnp.concatenate(
        [jnp.concatenate(
            [lax.dot_general(q16[psl(i)], pair_rows(k_refs[pg], i), (((1,), (1,)), ((), ())),
                             preferred_element_type=f32)
             for i in range(n_pairs)], axis=0) - jnp.repeat(c_pages[pg], SUBLANE, axis=0)
         for pg in range(pp)], axis=1)
    m = m_ref[...]
    m_new = jnp.maximum(m, jnp.max(s, axis=-1, keepdims=True))
    alpha = jnp.exp(m - m_new)
    p = jnp.exp(s - m_new)
    l_ref[...] = alpha * l_ref[...] + jnp.sum(p, axis=-1, keepdims=True)
    p16 = p.astype(bf16)
    pv = []
    for i in range(n_pairs):
        pv_i = jnp.zeros((2 * SUBLANE, 2 * HEAD_DIM), f32)
        for pg in range(pp):
            pv_i = pv_i + jnp.dot(p16[psl(i), pg * PAGE_SIZE:(pg + 1) * PAGE_SIZE], pair_rows(v_refs[pg], i),
                                  preferred_element_type=f32)
        pv += [pv_i[:SUBLANE, :HEAD_DIM], pv_i[SUBLANE:, HEAD_DIM:]]
    acc_ref[...] = alpha * acc_ref[...] + jnp.concatenate(pv, axis=0)
    m_ref[...] = m_new

    @pl.when(g == pl.num_programs(1) - 1)
    def _():
        c_last = jnp.concatenate(
            [jnp.broadcast_to(carry_ref[h:h + 1, 0:1], (SUBLANE, 1)) for h in range(N_HEADS)], axis=0)
        ct_row = c_last + cscol_ref[0]
        ct_new = c_last + csn_ref[0]
        qp = qp_ref[...]
        kn = kn_ref[0]
        vn = vn_ref[0]
        s_n = jnp.concatenate(
            [jnp.concatenate([jnp.sum(qp[rsl(h), half(h)] * kn[t:t + 1, hsl(h)], axis=-1, keepdims=True)
                              for t in range(t_len)], axis=1)
             for h in range(N_HEADS)], axis=0) - ct_new
        t_row = lax.broadcasted_iota(jnp.int32, (n_rows, t_len), 0) % SUBLANE
        t_col = lax.broadcasted_iota(jnp.int32, (n_rows, t_len), 1)
        s_n = jnp.where(t_col <= t_row, s_n, -jnp.inf)
        m = m_ref[...]
        m_new = jnp.maximum(m, jnp.max(s_n, axis=-1, keepdims=True))
        shift = (m_new + ct_row) - ct_row
        alpha = jnp.exp(m - shift)
        p = jnp.exp(s_n - shift)
        l = alpha * l_ref[...] + jnp.sum(p, axis=-1, keepdims=True)
        acc = alpha * acc_ref[...]
        for h in range(N_HEADS):
            acc_h = acc[rsl(h)]
            for t in range(t_len):
                acc_h = acc_h + p[rsl(h), t:t + 1] * vn[t:t + 1, hsl(h)]
            o_ref[0, :, hsl(h)] = (acc_h / l[rsl(h)])[:t_len] * jax.nn.silu(z_g_ref[0, :, hsl(h)])


DECODE_PAGES_PER_STEP = 16


def _attn_sample(z3, k3, v3, cscol, csn, cache_k4, cache_v4, cache_lft, page_table, layer):
    bsz, t_len, _ = z3.shape
    n_pages = page_table.shape[1]
    pp = min(DECODE_PAGES_PER_STEP, n_pages)
    assert n_pages % pp == 0 and t_len <= SUBLANE
    n_rows = N_HEADS * SUBLANE

    def page_spec(shape, pg):
        return pl.BlockSpec(shape, lambda b, g, pt: (layer, pt[b, g * pp + pg], 0, 0))

    z_blk = lambda c: pl.BlockSpec((1, t_len, D_ATTN), lambda b, g, pt: (b, 0, c))
    in_specs = [z_blk(Z_Q * PROJ_TN // D_ATTN), z_blk(Z_GA * PROJ_TN // D_ATTN), z_blk(0), z_blk(0),
                pl.BlockSpec((1, n_rows, 1), lambda b, g, pt: (b, 0, 0)),
                pl.BlockSpec((1, n_rows, t_len), lambda b, g, pt: (b, 0, 0))]
    in_specs += [page_spec((1, 1, PAGE_SIZE * N_HEADS, HEAD_DIM), pg) for pg in range(pp)]
    in_specs += [page_spec((1, 1, PAGE_SIZE * N_HEADS, HEAD_DIM), pg) for pg in range(pp)]
    in_specs += [page_spec((1, 1, N_HEADS, PAGE_SIZE), pg) for pg in range(pp)]
    grid_spec = pltpu.PrefetchScalarGridSpec(
        num_scalar_prefetch=1,
        grid=(bsz, n_pages // pp),
        in_specs=in_specs,
        out_specs=pl.BlockSpec((1, t_len, D_ATTN), lambda b, g, pt: (b, 0, 0)),
        scratch_shapes=[
            pltpu.VMEM((n_rows, 2 * HEAD_DIM), f32),
            pltpu.VMEM((n_rows, 1), f32),
            pltpu.VMEM((n_rows, 1), f32),
            pltpu.VMEM((n_rows, HEAD_DIM), f32),
            pltpu.VMEM((N_HEADS, LANE), f32),
        ],
    )
    return pl.pallas_call(
        functools.partial(_decode_kernel, pages_per_step=pp, t_len=t_len),
        grid_spec=grid_spec,
        out_shape=jax.ShapeDtypeStruct((bsz, t_len, D_ATTN), f32),
        compiler_params=_cparams("arbitrary", "arbitrary"),
        name="attn_sample",
    )(page_table, z3, z3, k3, v3, cscol, csn,
      *([cache_k4] * pp), *([cache_v4] * pp), *([cache_lft] * pp))


def _ln_silu_pw(y, lng, lnb, pw16):
    mu = jnp.mean(y, axis=-1, keepdims=True)
    yc = y - mu
    yn = yc * lax.rsqrt(jnp.mean(yc * yc, axis=-1, keepdims=True) + EPS) * lng + lnb
    return jnp.dot(jax.nn.silu(yn).astype(bf16), pw16, preferred_element_type=f32)


CONV_HALO = 32
CONV_CHUNK = 64
CONV_SHIFT_ROWS = CONV_HALO - SUBLANE


def _conv_prompt_kernel(u_ref, g_ref, w_ref, b_ref, lng_ref, lnb_ref, pw_ref, o_ref, st_ref, ext_ref, y_ref, sh_ref):
    i = pl.program_id(0)
    tr = u_ref.shape[0]
    pad = CONV_HALO - (CONV_WIDTH - 1)

    @pl.when(i == 0)
    def _():
        ext_ref[0:CONV_HALO, :] = jnp.zeros((CONV_HALO, D_CONV), f32)

    ext_ref[CONV_HALO:CONV_HALO + tr, :] = u_ref[...]
    for r in range(1, SUBLANE):
        sh_ref[r - 1] = ext_ref[r:r + tr + CONV_SHIFT_ROWS, :]
    for c in range(tr // CONV_CHUNK):
        r0 = c * CONV_CHUNK
        acc = jnp.broadcast_to(b_ref[...], (CONV_CHUNK, D_CONV))
        for w in range(CONV_WIDTH):
            a, r = divmod(pad + w, SUBLANE)
            lo = r0 + a * SUBLANE
            rows = ext_ref[lo:lo + CONV_CHUNK, :] if r == 0 else sh_ref[r - 1, lo:lo + CONV_CHUNK, :]
            acc = acc + rows * w_ref[w:w + 1, :]
        y_ref[r0:r0 + CONV_CHUNK, :] = acc
    o = _ln_silu_pw(y_ref[...], lng_ref[...], lnb_ref[...], pw_ref[...])
    o_ref[...] = (o * jax.nn.silu(g_ref[...])).astype(bf16)
    tail = ext_ref[tr:tr + CONV_HALO, :]
    ext_ref[0:CONV_HALO, :] = tail
    st_ref[...] = ext_ref[pad:CONV_HALO, :]


def _conv_prompt(z, dw_w, dw_b, lng, lnb, pw16):
    s_len = z.shape[0]
    tr = min(512, s_len)
    assert s_len % tr == 0 and tr % CONV_CHUNK == 0
    vec = pl.BlockSpec((1, D_CONV), lambda i: (0, 0))
    return pl.pallas_call(
        _conv_prompt_kernel,
        grid=(s_len // tr,),
        in_specs=[
            pl.BlockSpec((tr, D_CONV), lambda i: (i, Z_U)),
            pl.BlockSpec((tr, D_CONV), lambda i: (i, Z_GC)),
            pl.BlockSpec((CONV_WIDTH, D_CONV), lambda i: (0, 0)),
            vec, vec, vec,
            pl.BlockSpec((D_CONV, D_CONV), lambda i: (0, 0)),
        ],
        out_specs=[
            pl.BlockSpec((tr, D_CONV), lambda i: (i, 0)),
            pl.BlockSpec((CONV_WIDTH - 1, D_CONV), lambda i: (0, 0)),
        ],
        out_shape=[
            jax.ShapeDtypeStruct((s_len, D_CONV), bf16),
            jax.ShapeDtypeStruct((CONV_WIDTH - 1, D_CONV), f32),
        ],
        scratch_shapes=[
            pltpu.VMEM((tr + CONV_HALO, D_CONV), f32),
            pltpu.VMEM((tr, D_CONV), f32),
            pltpu.VMEM((SUBLANE - 1, tr + CONV_SHIFT_ROWS, D_CONV), f32),
        ],
        compiler_params=_cparams("arbitrary"),
        name="conv_prompt",
    )(z, z, dw_w, dw_b, lng, lnb, pw16)


def _conv_sample_kernel(st_ref, u_ref, g_ref, w_ref, b_ref, lng_ref, lnb_ref, pw_ref, o_ref, nst_ref):
    n_st = st_ref.shape[0]
    t_len = u_ref.shape[0]
    bsz = u_ref.shape[1]

    def ext(r):
        return st_ref[r] if r < n_st else u_ref[r - n_st]

    for t in range(t_len):
        acc = jnp.broadcast_to(b_ref[...], (bsz, D_CONV))
        for w in range(CONV_WIDTH):
            acc = acc + ext(t + w) * w_ref[w:w + 1, :]
        o_ref[t] = _ln_silu_pw(acc, lng_ref[...], lnb_ref[...], pw_ref[...]) * jax.nn.silu(g_ref[t])
    for r in range(n_st):
        nst_ref[r] = ext(r + t_len)


def _conv_sample(st_t, u_t, g_t, dw_w, dw_b, lng, lnb, pw16):
    t_len, bsz, _ = u_t.shape
    return pl.pallas_call(
        _conv_sample_kernel,
        out_shape=[
            jax.ShapeDtypeStruct((t_len, bsz, D_CONV), f32),
            jax.ShapeDtypeStruct(st_t.shape, f32),
        ],
        compiler_params=pltpu.CompilerParams(vmem_limit_bytes=VMEM_LIMIT),
        name="conv_sample",
    )(st_t, u_t, g_t, dw_w, dw_b, lng, lnb, pw16)


def _blockdiag(x16, w_ref):
    outs = [jnp.dot(x16[:, n * LRU_BLOCK:(n + 1) * LRU_BLOCK], w_ref[n], preferred_element_type=f32)
            for n in range(LRU_BLOCKS)]
    return jnp.concatenate(outs, axis=1)


def _lru_gates(xc, wa_ref, ba_ref, wx_ref, bx_ref, lam_ref):
    x16 = xc.astype(bf16)
    r = jax.nn.sigmoid(_blockdiag(x16, wa_ref) + ba_ref[...])
    i = jax.nn.sigmoid(_blockdiag(x16, wx_ref) + bx_ref[...])
    log_a = -LRU_C * r * jax.nn.softplus(-lam_ref[...])
    a = jnp.exp(log_a)
    one_minus_a2 = -jnp.tanh(log_a) * (a * a + 1.0)
    b = jnp.sqrt(one_minus_a2) * (i * xc)
    return a, b


LRU_HALO = 8


def _lru_prompt_kernel(x_ref, g_ref, cw_ref, cb_ref, wa_ref, ba_ref, wx_ref, bx_ref, lam_ref,
                       o_ref, st_ref, hl_ref, ext_ref, a_ref, b_ref, hs_ref, h_ref):
    i = pl.program_id(0)
    tr = x_ref.shape[0]
    pad = LRU_HALO - (LRU_CONV_WIDTH - 1)

    @pl.when(i == 0)
    def _():
        ext_ref[0:LRU_HALO, :] = jnp.zeros((LRU_HALO, D_LRU), f32)
        h_ref[...] = jnp.zeros_like(h_ref)

    ext_ref[LRU_HALO:LRU_HALO + tr, :] = x_ref[...]
    xc = jnp.broadcast_to(cb_ref[...], (tr, D_LRU))
    for w in range(LRU_CONV_WIDTH):
        xc = xc + ext_ref[pad + w:pad + w + tr, :] * cw_ref[w:w + 1, :]
    a, b = _lru_gates(xc, wa_ref, ba_ref, wx_ref, bx_ref, lam_ref)
    a_ref[...] = a
    b_ref[...] = b

    def body(t, h):
        h = a_ref[pl.ds(t, 1), :] * h + b_ref[pl.ds(t, 1), :]
        hs_ref[pl.ds(t, 1), :] = h
        return h

    h = lax.fori_loop(0, tr, body, h_ref[...], unroll=8)
    h_ref[...] = h
    hl_ref[...] = h
    o_ref[...] = (hs_ref[...] * jax.nn.silu(g_ref[...])).astype(bf16)
    tail = ext_ref[tr:tr + LRU_HALO, :]
    ext_ref[0:LRU_HALO, :] = tail
    st_ref[...] = ext_ref[pad:LRU_HALO, :]


def _lru_weight_specs(idx):
    vec = pl.BlockSpec((1, D_LRU), idx)
    blk = pl.BlockSpec((LRU_BLOCKS, LRU_BLOCK, LRU_BLOCK), lambda *a: (0, 0, 0))
    return [pl.BlockSpec((LRU_CONV_WIDTH, D_LRU), idx), vec, blk, vec, blk, vec, vec]


def _lru_prompt(z, lw):
    s_len = z.shape[0]
    tr = min(512, s_len)
    assert s_len % tr == 0
    return pl.pallas_call(
        _lru_prompt_kernel,
        grid=(s_len // tr,),
        in_specs=[pl.BlockSpec((tr, D_LRU), lambda i: (i, Z_LX)), pl.BlockSpec((tr, D_LRU), lambda i: (i, Z_GL))]
        + _lru_weight_specs(lambda i: (0, 0)),
        out_specs=[
            pl.BlockSpec((tr, D_LRU), lambda i: (i, 0)),
            pl.BlockSpec((LRU_CONV_WIDTH - 1, D_LRU), lambda i: (0, 0)),
            pl.BlockSpec((1, D_LRU), lambda i: (0, 0)),
        ],
        out_shape=[
            jax.ShapeDtypeStruct((s_len, D_LRU), bf16),
            jax.ShapeDtypeStruct((LRU_CONV_WIDTH - 1, D_LRU), f32),
            jax.ShapeDtypeStruct((1, D_LRU), f32),
        ],
        scratch_shapes=[
            pltpu.VMEM((tr + LRU_HALO, D_LRU), f32),
            pltpu.VMEM((tr, D_LRU), f32),
            pltpu.VMEM((tr, D_LRU), f32),
            pltpu.VMEM((tr, D_LRU), f32),
            pltpu.VMEM((1, D_LRU), f32),
        ],
        compiler_params=_cparams("arbitrary"),
        name="lru_prompt",
    )(z, z, *lw)


def _lru_sample_kernel(st_ref, x_ref, g_ref, h0_ref, cw_ref, cb_ref, wa_ref, ba_ref, wx_ref, bx_ref, lam_ref,
                       o_ref, nst_ref, hl_ref):
    n_st = st_ref.shape[0]
    t_len = x_ref.shape[0]
    bsz = x_ref.shape[1]

    def ext(r):
        return st_ref[r] if r < n_st else x_ref[r - n_st]

    h = h0_ref[...]
    for t in range(t_len):
        xc = jnp.broadcast_to(cb_ref[...], (bsz, D_LRU))
        for w in range(LRU_CONV_WIDTH):
            xc = xc + ext(t + w) * cw_ref[w:w + 1, :]
        a, b = _lru_gates(xc, wa_ref, ba_ref, wx_ref, bx_ref, lam_ref)
        h = a * h + b
        o_ref[t] = h * jax.nn.silu(g_ref[t])
    hl_ref[...] = h
    for r in range(n_st):
        nst_ref[r] = ext(r + t_len)


def _lru_sample(st_t, x_t, g_t, h0, lw):
    t_len, bsz, _ = x_t.shape
    return pl.pallas_call(
        _lru_sample_kernel,
        out_shape=[
            jax.ShapeDtypeStruct((t_len, bsz, D_LRU), f32),
            jax.ShapeDtypeStruct(st_t.shape, f32),
            jax.ShapeDtypeStruct((bsz, D_LRU), f32),
        ],
        compiler_params=pltpu.CompilerParams(vmem_limit_bytes=VMEM_LIMIT),
        name="lru_sample",
    )(st_t, x_t, g_t, h0, *lw)


MERGE_TM = 1024
MERGE_TN = 1024


def _merge_kernel(ya_ref, yc_ref, yl_ref, x_ref, w_ref, o_ref):
    acc = jnp.dot(ya_ref[...], w_ref[0:D_ATTN, :], preferred_element_type=f32)
    acc = acc + jnp.dot(yc_ref[...], w_ref[D_ATTN:D_ATTN + D_CONV, :], preferred_element_type=f32)
    acc = acc + jnp.dot(yl_ref[...], w_ref[D_ATTN + D_CONV:, :], preferred_element_type=f32)
    o_ref[...] = x_ref[...] + acc


def _merge(x2d, y_attn, y_conv, y_lru, w_out16):
    n = x2d.shape[0]
    tm = min(MERGE_TM, n)
    assert n % tm == 0
    return pl.pallas_call(
        _merge_kernel,
        grid=(n // tm, D_MODEL // MERGE_TN),
        in_specs=[
            pl.BlockSpec((tm, D_ATTN), lambda i, j: (i, 0)),
            pl.BlockSpec((tm, D_CONV), lambda i, j: (i, 0)),
            pl.BlockSpec((tm, D_LRU), lambda i, j: (i, 0)),
            pl.BlockSpec((tm, MERGE_TN), lambda i, j: (i, j)),
            pl.BlockSpec((D_MODEL, MERGE_TN), lambda i, j: (0, j)),
        ],
        out_specs=pl.BlockSpec((tm, MERGE_TN), lambda i, j: (i, j)),
        out_shape=jax.ShapeDtypeStruct((n, D_MODEL), f32),
        compiler_params=_cparams("arbitrary", "arbitrary"),
        name="merge",
    )(y_attn, y_conv, y_lru, x2d, w_out16)


W_IN_SEGMENTS = (("q", D_ATTN), ("k", D_ATTN), ("v", D_ATTN), ("f", N_HEADS), ("ga", D_ATTN),
                 ("glu_a", D_CONV), ("glu_b", D_CONV), ("gc", D_CONV), ("lx", D_LRU), ("gl", D_LRU))
W_MAIN_ORDER = ("q", "k", "v", "ga", "gc", "lx", "gl", "glu_b", "glu_a")
D_IN = sum(size for _, size in W_IN_SEGMENTS)
PREP_TR = 256


def _prep_w_in_kernel(wt_ref, wm_ref, wf_ref):
    src = {}
    o = 0
    for name, size in W_IN_SEGMENTS:
        src[name] = (o, size)
        o += size
    d = 0
    for name in W_MAIN_ORDER:
        o, size = src[name]
        wm_ref[:, d:d + size] = wt_ref[0, o:o + size, :].T.astype(bf16)
        d += size
    o, size = src["f"]
    wf_ref[...] = jnp.zeros_like(wf_ref)
    wf_ref[:, 0:size] = wt_ref[0, o:o + size, :].T.astype(bf16)


def _prep_w_in(w_in_t, layer):
    assert all(sum(size for _, size in W_IN_SEGMENTS[:n]) % SUBLANE == 0 for n in range(len(W_IN_SEGMENTS)))
    return pl.pallas_call(
        _prep_w_in_kernel,
        grid=(D_MODEL // PREP_TR,),
        in_specs=[pl.BlockSpec((1, D_IN, PREP_TR), lambda i: (layer, 0, i))],
        out_specs=[pl.BlockSpec((PREP_TR, W_COLS), lambda i: (i, 0)), pl.BlockSpec((PREP_TR, LANE), lambda i: (i, 0))],
        out_shape=[jax.ShapeDtypeStruct((D_MODEL, W_COLS), bf16), jax.ShapeDtypeStruct((D_MODEL, LANE), bf16)],
        compiler_params=_cparams("arbitrary"),
        name="prep_w_in",
    )(w_in_t)


def kernel(x_prompt, x_sample, cache_k, cache_v, cache_logf, state_conv, state_lru_conv, state_lru_h,
           page_table, norm_g, w_in, b_f, q_norm_g, k_norm_g, conv_dw_w, conv_dw_b, conv_ln_g, conv_ln_b,
           conv_pw_w, lru_conv_w, lru_conv_b, lru_w_a, lru_b_a, lru_w_x, lru_b_x, lru_lambda, w_out):
    depth = w_in.shape[0]
    n_seq_p, s_len, _ = x_prompt.shape
    bsz, t_len, _ = x_sample.shape
    assert n_seq_p == 1, "the prompt kernels carry one sequence through their row tiles"
    n_pool = cache_k.shape[1]
    cache_k4 = cache_k.reshape(depth, n_pool, PAGE_SIZE * N_HEADS, HEAD_DIM)
    cache_v4 = cache_v.reshape(depth, n_pool, PAGE_SIZE * N_HEADS, HEAD_DIM)
    cache_lft = jnp.swapaxes(cache_logf, 2, 3)
    w_in_t = jnp.swapaxes(w_in, 1, 2)

    xp = x_prompt.reshape(s_len, D_MODEL)
    xs = x_sample.reshape(bsz * t_len, D_MODEL)
    outs = {k: [] for k in ("pk", "pv", "pf", "pc", "plc", "ph", "sk", "sv", "sf", "sc", "slc", "sh")}
    row = lambda v: v.reshape(1, -1)
    for l in range(depth):
        wm, wf = _prep_w_in(w_in_t, l)
        bfp = jnp.pad(b_f[l], (0, LANE - N_HEADS)).reshape(1, LANE)
        proj_w = (row(norm_g[l]), wm, wf, bfp, row(q_norm_g[l]), row(k_norm_g[l]))
        conv_w = (conv_dw_w[l], row(conv_dw_b[l]), row(conv_ln_g[l]), row(conv_ln_b[l]), conv_pw_w[l].astype(bf16))
        lru_w = (lru_conv_w[l], row(lru_conv_b[l]), lru_w_a[l].astype(bf16), row(lru_b_a[l]),
                 lru_w_x[l].astype(bf16), row(lru_b_x[l]), row(lru_lambda[l]))
        w_out16 = w_out[l].astype(bf16)

        z, k_p, v_p, qkv16, logf, cs = _project(xp, *proj_w, seq_rows=s_len, carry_over=True)
        y_att = _attn_prompt(qkv16, cs[:, :N_HEADS].T.reshape(N_HEADS, 1, s_len), cs, z)
        y_conv, st_c = _conv_prompt(z, *conv_w)
        y_lru, st_l, h_l = _lru_prompt(z, lru_w)
        xp = _merge(xp, y_att, y_conv, y_lru, w_out16)
        outs["pk"].append(k_p.reshape(1, s_len, N_HEADS, HEAD_DIM))
        outs["pv"].append(v_p.reshape(1, s_len, N_HEADS, HEAD_DIM))
        outs["pf"].append(logf[:, :N_HEADS].reshape(1, s_len, N_HEADS))
        outs["pc"].append(st_c[None])
        outs["plc"].append(st_l[None])
        outs["ph"].append(h_l)

        z, k_s, v_s, _, logf, cs = _project(xs, *proj_w, seq_rows=t_len, carry_over=False)
        z3 = z.reshape(bsz, t_len, Z_COLS)
        k3 = k_s.reshape(bsz, t_len, D_ATTN)
        v3 = v_s.reshape(bsz, t_len, D_ATTN)
        cs3 = cs[:, :N_HEADS].reshape(bsz, t_len, N_HEADS)
        cs_ht = jnp.swapaxes(cs3, 1, 2)
        cscol = jnp.pad(cs_ht, ((0, 0), (0, 0), (0, SUBLANE - t_len))).reshape(bsz, N_HEADS * SUBLANE, 1)
        csn = jnp.repeat(cs_ht, SUBLANE, axis=1)
        y_att = _attn_sample(z3, k3, v3, cscol, csn, cache_k4, cache_v4, cache_lft, page_table, l)
        tmaj = lambda a: jnp.swapaxes(a, 0, 1)
        zcol_t = lambda c: tmaj(z3[:, :, c * PROJ_TN:(c + 1) * PROJ_TN])
        bmaj16 = lambda a_t: tmaj(a_t).reshape(bsz * t_len, -1).astype(bf16)
        y_conv_t, nst_c = _conv_sample(tmaj(state_conv[l]), zcol_t(Z_U), zcol_t(Z_GC), *conv_w)
        y_lru_t, nst_l, h_l = _lru_sample(tmaj(state_lru_conv[l]), zcol_t(Z_LX), zcol_t(Z_GL), state_lru_h[l], lru_w)
        xs = _merge(xs, y_att.reshape(bsz * t_len, D_ATTN).astype(bf16), bmaj16(y_conv_t), bmaj16(y_lru_t), w_out16)
        outs["sk"].append(k3.reshape(bsz, t_len, N_HEADS, HEAD_DIM))
        outs["sv"].append(v3.reshape(bsz, t_len, N_HEADS, HEAD_DIM))
        outs["sf"].append(logf[:, :N_HEADS].reshape(bsz, t_len, N_HEADS))
        outs["sc"].append(tmaj(nst_c))
        outs["slc"].append(tmaj(nst_l))
        outs["sh"].append(h_l)

    st = {k: jnp.stack(v) for k, v in outs.items()}
    return (xp.reshape(1, s_len, D_MODEL), xs.reshape(bsz, t_len, D_MODEL),
            st["pk"], st["pv"], st["pf"], st["pc"], st["plc"], st["ph"],
            st["sk"], st["sv"], st["sf"], st["sc"], st["slc"], st["sh"])
```

```python
import functools
import math

import jax
import jax.numpy as jnp
from jax import lax
from jax.experimental import pallas as pl
from jax.experimental.pallas import tpu as pltpu

f32 = jnp.float32
bf16 = jnp.bfloat16

D_MODEL = 2048
HEAD_DIM = 128
N_HEADS = 8
D_ATTN = N_HEADS * HEAD_DIM
D_CONV = 512
CONV_WIDTH = 31
D_LRU = 512
LRU_BLOCKS = 4
LRU_BLOCK = D_LRU // LRU_BLOCKS
LRU_CONV_WIDTH = 4
LRU_C = 8.0
PAGE_SIZE = 128
EPS = 1e-6
SCALE = HEAD_DIM ** -0.5
LOG2E = math.log2(math.e)

LANE = 128
SUBLANE = 8
VMEM_LIMIT = 48 * 1024 * 1024

PROJ_TN = 512
PROJ_TM = 1024
CUMSUM_ROWS = 256
PROJ_SPLIT = 2
W_Q, W_K, W_V, W_GA, W_GC, W_LX, W_GL, W_SIG, W_U = 0, 2, 4, 6, 8, 9, 10, 11, 12
W_COLS = 13 * PROJ_TN
Z_Q, Z_GA, Z_GC, Z_LX, Z_GL, Z_SIG, Z_U = 0, 2, 4, 5, 6, 7, 8
Z_COLS = 9 * PROJ_TN
Z_SHIFT = W_GA - Z_GA


def _cparams(*sem):
    return pltpu.CompilerParams(dimension_semantics=sem, vmem_limit_bytes=VMEM_LIMIT)


def _split3(x):
    hi = x.astype(bf16).astype(f32)
    r1 = x - hi
    mid = r1.astype(bf16).astype(f32)
    lo = (r1 - mid).astype(bf16).astype(f32)
    return hi, mid, lo


def _head_rmsnorm(z, g):
    outs = []
    for h in range(PROJ_TN // HEAD_DIM):
        zh = z[:, h * HEAD_DIM:(h + 1) * HEAD_DIM]
        ms = jnp.mean(zh * zh, axis=-1, keepdims=True)
        outs.append(zh * lax.rsqrt(ms + EPS) * g)
    return jnp.concatenate(outs, axis=1)


def _proj_kernel(x_ref, g_ref, wm_ref, wf_ref, bf_ref, qg_ref, kg_ref,
                 z_ref, k_ref, v_ref, qkv_ref, logf_ref, cs_ref,
                 xn_ref, sig_ref, carry_ref, *, seq_rows, carry_over):
    i = pl.program_id(0)
    j = pl.program_id(1)
    tm = x_ref.shape[0]

    @pl.when(j == 0)
    def _():
        tb = min(tm, CUMSUM_ROWS)
        ii = lax.broadcasted_iota(jnp.int32, (tb, tb), 0)
        jj = lax.broadcasted_iota(jnp.int32, (tb, tb), 1)
        keep = jj <= ii
        if not carry_over:
            keep = jnp.logical_and(keep, (ii // seq_rows) == (jj // seq_rows))
        tri = jnp.where(keep, 1.0, 0.0).astype(bf16)
        if carry_over:
            @pl.when(i == 0)
            def _():
                carry_ref[...] = jnp.zeros_like(carry_ref)
        for r0 in range(0, tm, tb):
            x = x_ref[r0:r0 + tb, :]
            y = x * lax.rsqrt(jnp.mean(x * x, axis=-1, keepdims=True) + EPS) * g_ref[...]
            xn = y.astype(bf16)
            xn_ref[r0:r0 + tb, :] = xn
            f = jnp.dot(xn, wf_ref[...], preferred_element_type=f32) + bf_ref[...]
            logf = jax.nn.log_sigmoid(f)
            logf_ref[r0:r0 + tb, :] = logf
            parts = jnp.concatenate(_split3(logf), axis=1).astype(bf16)
            c3 = jnp.dot(tri, parts, preferred_element_type=f32)
            cs = c3[:, :LANE] + c3[:, LANE:2 * LANE] + c3[:, 2 * LANE:]
            if carry_over:
                cs = cs + carry_ref[...]
                carry_ref[...] = cs[tb - 1:tb, :]
            cs_ref[r0:r0 + tb, :] = cs

    halves = [(r0, tm // PROJ_SPLIT) for r0 in range(0, tm, tm // PROJ_SPLIT)]

    def tiles():
        return [(slice(r0, r0 + n), jnp.dot(xn_ref[r0:r0 + n, :], wm_ref[...], preferred_element_type=f32))
                for r0, n in halves]

    @pl.when(j < W_K)
    def _():
        for rows, z in tiles():
            qn = _head_rmsnorm(z, qg_ref[...])
            z_ref[rows, :] = qn
            qkv_ref[rows, :] = (qn * (SCALE * LOG2E)).astype(bf16)

    @pl.when(jnp.logical_and(j >= W_K, j < W_V))
    def _():
        for rows, z in tiles():
            kn = _head_rmsnorm(z, kg_ref[...])
            k_ref[rows, :] = kn
            qkv_ref[rows, :] = kn.astype(bf16)

    @pl.when(jnp.logical_and(j >= W_V, j < W_GA))
    def _():
        for rows, z in tiles():
            v_ref[rows, :] = z
            qkv_ref[rows, :] = z.astype(bf16)

    @pl.when(jnp.logical_and(j >= W_GA, j < W_SIG))
    def _():
        for rows, z in tiles():
            z_ref[rows, :] = z

    @pl.when(j == W_SIG)
    def _():
        for rows, z in tiles():
            s = jax.nn.sigmoid(z)
            sig_ref[rows, :] = s
            z_ref[rows, :] = s

    @pl.when(j == W_U)
    def _():
        for rows, z in tiles():
            z_ref[rows, :] = z * sig_ref[rows, :]


def _project(x2d, norm_g, wm, wf, bfp, qg, kg, *, seq_rows, carry_over):
    n = x2d.shape[0]
    tm = min(PROJ_TM, n)
    tb = min(tm, CUMSUM_ROWS)
    assert n % tm == 0 and tm % tb == 0 and (carry_over or tb % seq_rows == 0)
    grid = (n // tm, W_COLS // PROJ_TN)
    clamp = lambda j, lo, hi: jnp.minimum(jnp.maximum(j, lo), hi) - lo
    kern = functools.partial(_proj_kernel, seq_rows=seq_rows, carry_over=carry_over)
    return pl.pallas_call(
        kern,
        grid=grid,
        in_specs=[
            pl.BlockSpec((tm, D_MODEL), lambda i, j: (i, 0)),
            pl.BlockSpec((1, D_MODEL), lambda i, j: (0, 0)),
            pl.BlockSpec((D_MODEL, PROJ_TN), lambda i, j: (0, j)),
            pl.BlockSpec((D_MODEL, LANE), lambda i, j: (0, 0)),
            pl.BlockSpec((1, LANE), lambda i, j: (0, 0)),
            pl.BlockSpec((1, HEAD_DIM), lambda i, j: (0, 0)),
            pl.BlockSpec((1, HEAD_DIM), lambda i, j: (0, 0)),
        ],
        out_specs=[
            pl.BlockSpec((tm, PROJ_TN), lambda i, j: (i, jnp.where(j < W_K, j, jnp.maximum(j - Z_SHIFT, Z_GA)))),
            pl.BlockSpec((tm, PROJ_TN), lambda i, j: (i, clamp(j, W_K, W_V - 1))),
            pl.BlockSpec((tm, PROJ_TN), lambda i, j: (i, clamp(j, W_V, W_GA - 1))),
            pl.BlockSpec((tm, PROJ_TN), lambda i, j: (i, clamp(j, W_Q, W_GA - 1))),
            pl.BlockSpec((tm, LANE), lambda i, j: (i, 0)),
            pl.BlockSpec((tm, LANE), lambda i, j: (i, 0)),
        ],
        out_shape=[
            jax.ShapeDtypeStruct((n, Z_COLS), f32),
            jax.ShapeDtypeStruct((n, D_ATTN), f32),
            jax.ShapeDtypeStruct((n, D_ATTN), f32),
            jax.ShapeDtypeStruct((n, 3 * D_ATTN), bf16),
            jax.ShapeDtypeStruct((n, LANE), f32),
            jax.ShapeDtypeStruct((n, LANE), f32),
        ],
        scratch_shapes=[
            pltpu.VMEM((tm, D_MODEL), bf16),
            pltpu.VMEM((tm, PROJ_TN), f32),
            pltpu.VMEM((1, LANE), f32),
        ],
        compiler_params=_cparams("arbitrary", "arbitrary"),
        name="proj",
    )(x2d, norm_g, wm, wf, bfp, qg, kg)


ATTN_TQ = 1024


def _attn_prompt_kernel(q_ref, qn_ref, k_ref, v_ref, crow_ref, cs_ref, g_ref, o_ref, s_ref, p_ref, m_ref, acc_ref):
    qi = pl.program_id(1)
    tq = q_ref.shape[0]
    q = q_ref[...]
    lane = lax.broadcasted_iota(jnp.int32, cs_ref.shape, 1)
    cq = jnp.sum(jnp.where(lane == pl.program_id(0), cs_ref[...], 0.0), axis=-1, keepdims=True) * LOG2E

    def chunk(kj):
        return pl.ds(pl.multiple_of(kj * tq, tq), tq)

    def qk_into_s(kj, q=q):
        s_ref[...] = lax.dot_general(q, k_ref[chunk(kj), :], (((1,), (1,)), ((), ())),
                                     preferred_element_type=f32)

    ones = jnp.ones((tq, HEAD_DIM), bf16)

    def pv(kj):
        v_aug = jnp.concatenate([v_ref[chunk(kj), :], ones], axis=1)
        return jnp.dot(p_ref[...], v_aug, preferred_element_type=f32)

    def softmax_of_s(kj, masked):
        s = s_ref[...] - crow_ref[0, :, chunk(kj)] * LOG2E
        if masked:
            row = lax.broadcasted_iota(jnp.int32, (tq, tq), 0)
            col = lax.broadcasted_iota(jnp.int32, (tq, tq), 1)
            s = jnp.where(col <= row, s, -jnp.inf)
        m = m_ref[...]
        m_new = jnp.maximum(m, jnp.max(s, axis=-1, keepdims=True))
        shift = (m_new + cq) - cq
        alpha = jnp.exp2(m - shift)
        p = jnp.exp2(s - shift)
        m_ref[...] = shift
        return p.astype(bf16), alpha

    m_ref[...] = jnp.full_like(m_ref, -jnp.inf)
    acc_ref[...] = jnp.zeros_like(acc_ref)
    p_ref[...] = jnp.zeros_like(p_ref)

    @pl.when(qi == 0)
    def _():
        qk_into_s(0)

    def body(j, _):
        pv_prev = pv(jnp.maximum(j - 1, 0))
        p, alpha = softmax_of_s(j, False)
        qk_into_s(j + 1)
        p_ref[...] = p
        acc_ref[...] = alpha * (pv_prev + acc_ref[...])
        return 0

    lax.fori_loop(0, qi, body, 0)
    pv_prev = pv(jnp.maximum(qi - 1, 0))
    p, alpha = softmax_of_s(qi, True)
    qk_into_s(0, qn_ref[...])
    p_ref[...] = p
    acc = pv(qi) + alpha * (pv_prev + acc_ref[...])
    o_ref[...] = ((acc[:, :HEAD_DIM] / acc[:, HEAD_DIM:]) * jax.nn.silu(g_ref[...])).astype(bf16)


def _attn_prompt(qkv16, crow, cs, z):
    s_len = qkv16.shape[0]
    tq = min(ATTN_TQ, s_len)
    assert s_len % tq == 0
    n_q = s_len // tq
    return pl.pallas_call(
        _attn_prompt_kernel,
        grid=(N_HEADS, n_q),
        scratch_shapes=[
            pltpu.VMEM((tq, tq), f32),
            pltpu.VMEM((tq, tq), bf16),
            pltpu.VMEM((tq, 1), f32),
            pltpu.VMEM((tq, 2 * HEAD_DIM), f32),
        ],
        in_specs=[
            pl.BlockSpec((tq, HEAD_DIM), lambda h, i: (i, h)),
            pl.BlockSpec((tq, HEAD_DIM), lambda h, i: (jnp.minimum(i + 1, n_q - 1), h)),
            pl.BlockSpec((s_len, HEAD_DIM), lambda h, i: (0, N_HEADS + h)),
            pl.BlockSpec((s_len, HEAD_DIM), lambda h, i: (0, 2 * N_HEADS + h)),
            pl.BlockSpec((1, 1, s_len), lambda h, i: (h, 0, 0)),
            pl.BlockSpec((tq, LANE), lambda h, i: (i, 0)),
            pl.BlockSpec((tq, HEAD_DIM), lambda h, i: (i, Z_GA * PROJ_TN // HEAD_DIM + h)),
        ],
        out_specs=pl.BlockSpec((tq, HEAD_DIM), lambda h, i: (i, h)),
        out_shape=jax.ShapeDtypeStruct((s_len, D_ATTN), bf16),
        compiler_params=_cparams("arbitrary", "arbitrary"),
        name="attn_prompt",
    )(qkv16, qkv16, qkv16, qkv16, crow, cs, z)


def _decode_kernel(pt_ref, z_q_ref, z_g_ref, kn_ref, vn_ref, cscol_ref, csn_ref, *rest, pages_per_step, t_len):
    pp = pages_per_step
    k_refs = rest[:pp]
    v_refs = rest[pp:2 * pp]
    lf_refs = rest[2 * pp:3 * pp]
    o_ref = rest[3 * pp]
    qp_ref, m_ref, l_ref, acc_ref, carry_ref = rest[3 * pp + 1:]
    g = pl.program_id(1)
    n_rows = N_HEADS * SUBLANE
    n_pairs = N_HEADS // 2
    hsl = lambda h: slice(h * HEAD_DIM, (h + 1) * HEAD_DIM)
    rsl = lambda h: slice(h * SUBLANE, (h + 1) * SUBLANE)
    psl = lambda i: slice(2 * i * SUBLANE, 2 * (i + 1) * SUBLANE)
    half = lambda h: slice((h % 2) * HEAD_DIM, (h % 2 + 1) * HEAD_DIM)

    @pl.when(g == 0)
    def _():
        qp_ref[...] = jnp.zeros_like(qp_ref)
        q = z_q_ref[0] * SCALE
        for h in range(N_HEADS):
            qp_ref[h * SUBLANE:h * SUBLANE + t_len, half(h)] = q[:, hsl(h)]
        m_ref[...] = jnp.full_like(m_ref, -jnp.inf)
        l_ref[...] = jnp.zeros_like(l_ref)
        acc_ref[...] = jnp.zeros_like(acc_ref)
        carry_ref[...] = jnp.zeros_like(carry_ref)

    ii = lax.broadcasted_iota(jnp.int32, (PAGE_SIZE, PAGE_SIZE), 0)
    jj = lax.broadcasted_iota(jnp.int32, (PAGE_SIZE, PAGE_SIZE), 1)
    upper = jnp.where(ii <= jj, 1.0, 0.0).astype(bf16)
    q16 = qp_ref[...].astype(bf16)

    head_rows = lambda ref, h: ref[0, 0, pl.ds(h, PAGE_SIZE, stride=N_HEADS), :].astype(bf16)
    c_pages = []
    carry = carry_ref[...]
    for pg in range(pp):
        parts = jnp.concatenate(_split3(lf_refs[pg][0, 0]), axis=0).astype(bf16)
        c3 = jnp.dot(parts, upper, preferred_element_type=f32)
        c_page = (c3[:N_HEADS] + c3[N_HEADS:2 * N_HEADS] + c3[2 * N_HEADS:]) + carry
        carry = jnp.broadcast_to(c_page[:, PAGE_SIZE - 1:PAGE_SIZE], (N_HEADS, LANE))
        c_pages.append(c_page)
    carry_ref[...] = carry
    pair_rows = lambda ref, i: jnp.concatenate([head_rows(ref, 2 * i), head_rows(ref, 2 * i + 1)], axis=1)
    s = jnp.concatenate(
        [jnp.concatenate(
            [lax.dot_general(q16[psl(i)], pair_rows(k_refs[pg], i), (((1,), (1,)), ((), ())),
                             preferred_element_type=f32)
             for i in range(n_pairs)], axis=0) - jnp.repeat(c_pages[pg], SUBLANE, axis=0)
         for pg in range(pp)], axis=1)
    m = m_ref[...]
    m_new = jnp.maximum(m, jnp.max(s, axis=-1, keepdims=True))
    alpha = jnp.exp(m - m_new)
    p = jnp.exp(s - m_new)
    l_ref[...] = alpha * l_ref[...] + jnp.sum(p, axis=-1, keepdims=True)
    p16 = p.astype(bf16)
    pv = []
    for i in range(n_pairs):
        pv_i = jnp.zeros((2 * SUBLANE, 2 * HEAD_DIM), f32)
        for pg in range(pp):
            pv_i = pv_i + jnp.dot(p16[psl(i), pg * PAGE_SIZE:(pg + 1) * PAGE_SIZE], pair_rows(v_refs[pg], i),
                                  preferred_element_type=f32)
        pv += [pv_i[:SUBLANE, :HEAD_DIM], pv_i[SUBLANE:, HEAD_DIM:]]
    acc_ref[...] = alpha * acc_ref[...] + jnp.concatenate(pv, axis=0)
    m_ref[...] = m_new

    @pl.when(g == pl.num_programs(1) - 1)
    def _():
        c_last = jnp.concatenate(
            [jnp.broadcast_to(carry_ref[h:h + 1, 0:1], (SUBLANE, 1)) for h in range(N_HEADS)], axis=0)
        ct_row = c_last + cscol_ref[0]
        ct_new = c_last + csn_ref[0]
        qp = qp_ref[...]
        kn = kn_ref[0]
        vn = vn_ref[0]
        s_n = jnp.concatenate(
            [jnp.concatenate([jnp.sum(qp[rsl(h), half(h)] * kn[t:t + 1, hsl(h)], axis=-1, keepdims=True)
                              for t in range(t_len)], axis=1)
             for h in range(N_HEADS)], axis=0) - ct_new
        t_row = lax.broadcasted_iota(jnp.int32, (n_rows, t_len), 0) % SUBLANE
        t_col = lax.broadcasted_iota(jnp.int32, (n_rows, t_len), 1)
        s_n = jnp.where(t_col <= t_row, s_n, -jnp.inf)
        m = m_ref[...]
        m_new = jnp.maximum(m, jnp.max(s_n, axis=-1, keepdims=True))
        shift = (m_new + ct_row) - ct_row
        alpha = jnp.exp(m - shift)
        p = jnp.exp(s_n - shift)
        l = alpha * l_ref[...] + jnp.sum(p, axis=-1, keepdims=True)
        acc = alpha * acc_ref[...]
        for h in range(N_HEADS):
            acc_h = acc[rsl(h)]
            for t in range(t_len):
                acc_h = acc_h + p[rsl(h), t:t + 1] * vn[t:t + 1, hsl(h)]
            o_ref[0, :, hsl(h)] = (acc_h / l[rsl(h)])[:t_len] * jax.nn.silu(z_g_ref[0, :, hsl(h)])


DECODE_PAGES_PER_STEP = 16


def _attn_sample(z3, k3, v3, cscol, csn, cache_k4, cache_v4, cache_lft, page_table, layer):
    bsz, t_len, _ = z3.shape
    n_pages = page_table.shape[1]
    pp = min(DECODE_PAGES_PER_STEP, n_pages)
    assert n_pages % pp == 0 and t_len <= SUBLANE
    n_rows = N_HEADS * SUBLANE

    def page_spec(shape, pg):
        return pl.BlockSpec(shape, lambda b, g, pt: (layer, pt[b, g * pp + pg], 0, 0))

    z_blk = lambda c: pl.BlockSpec((1, t_len, D_ATTN), lambda b, g, pt: (b, 0, c))
    in_specs = [z_blk(Z_Q * PROJ_TN // D_ATTN), z_blk(Z_GA * PROJ_TN // D_ATTN), z_blk(0), z_blk(0),
                pl.BlockSpec((1, n_rows, 1), lambda b, g, pt: (b, 0, 0)),
                pl.BlockSpec((1, n_rows, t_len), lambda b, g, pt: (b, 0, 0))]
    in_specs += [page_spec((1, 1, PAGE_SIZE * N_HEADS, HEAD_DIM), pg) for pg in range(pp)]
    in_specs += [page_spec((1, 1, PAGE_SIZE * N_HEADS, HEAD_DIM), pg) for pg in range(pp)]
    in_specs += [page_spec((1, 1, N_HEADS, PAGE_SIZE), pg) for pg in range(pp)]
    grid_spec = pltpu.PrefetchScalarGridSpec(
        num_scalar_prefetch=1,
        grid=(bsz, n_pages // pp),
        in_specs=in_specs,
        out_specs=pl.BlockSpec((1, t_len, D_ATTN), lambda b, g, pt: (b, 0, 0)),
        scratch_shapes=[
            pltpu.VMEM((n_rows, 2 * HEAD_DIM), f32),
            pltpu.VMEM((n_rows, 1), f32),
            pltpu.VMEM((n_rows, 1), f32),
            pltpu.VMEM((n_rows, HEAD_DIM), f32),
            pltpu.VMEM((N_HEADS, LANE), f32),
        ],
    )
    return pl.pallas_call(
        functools.partial(_decode_kernel, pages_per_step=pp, t_len=t_len),
        grid_spec=grid_spec,
        out_shape=jax.ShapeDtypeStruct((bsz, t_len, D_ATTN), f32),
        compiler_params=_cparams("arbitrary", "arbitrary"),
        name="attn_sample",
    )(page_table, z3, z3, k3, v3, cscol, csn,
      *([cache_k4] * pp), *([cache_v4] * pp), *([cache_lft] * pp))


def _ln_silu_pw(y, lng, lnb, pw16):
    mu = jnp.mean(y, axis=-1, keepdims=True)
    yc = y - mu
    yn = yc * lax.rsqrt(jnp.mean(yc * yc, axis=-1, keepdims=True) + EPS) * lng + lnb
    return jnp.dot(jax.nn.silu(yn).astype(bf16), pw16, preferred_element_type=f32)


CONV_HALO = 32
CONV_CHUNK = 64
CONV_SHIFT_ROWS = CONV_HALO - SUBLANE


def _conv_prompt_kernel(u_ref, g_ref, w_ref, b_ref, lng_ref, lnb_ref, pw_ref, o_ref, st_ref, ext_ref, y_ref, sh_ref):
    i = pl.program_id(0)
    tr = u_ref.shape[0]
    pad = CONV_HALO - (CONV_WIDTH - 1)

    @pl.when(i == 0)
    def _():
        ext_ref[0:CONV_HALO, :] = jnp.zeros((CONV_HALO, D_CONV), f32)

    ext_ref[CONV_HALO:CONV_HALO + tr, :] = u_ref[...]
    for r in range(1, SUBLANE):
        sh_ref[r - 1] = ext_ref[r:r + tr + CONV_SHIFT_ROWS, :]
    for c in range(tr // CONV_CHUNK):
        r0 = c * CONV_CHUNK
        acc = jnp.broadcast_to(b_ref[...], (CONV_CHUNK, D_CONV))
        for w in range(CONV_WIDTH):
            a, r = divmod(pad + w, SUBLANE)
            lo = r0 + a * SUBLANE
            rows = ext_ref[lo:lo + CONV_CHUNK, :] if r == 0 else sh_ref[r - 1, lo:lo + CONV_CHUNK, :]
            acc = acc + rows * w_ref[w:w + 1, :]
        y_ref[r0:r0 + CONV_CHUNK, :] = acc
    o = _ln_silu_pw(y_ref[...], lng_ref[...], lnb_ref[...], pw_ref[...])
    o_ref[...] = (o * jax.nn.silu(g_ref[...])).astype(bf16)
    tail = ext_ref[tr:tr + CONV_HALO, :]
    ext_ref[0:CONV_HALO, :] = tail
    st_ref[...] = ext_ref[pad:CONV_HALO, :]


def _conv_prompt(z, dw_w, dw_b, lng, lnb, pw16):
    s_len = z.shape[0]
    tr = min(512, s_len)
    assert s_len % tr == 0 and tr % CONV_CHUNK == 0
    vec = pl.BlockSpec((1, D_CONV), lambda i: (0, 0))
    return pl.pallas_call(
        _conv_prompt_kernel,
        grid=(s_len // tr,),
        in_specs=[
            pl.BlockSpec((tr, D_CONV), lambda i: (i, Z_U)),
            pl.BlockSpec((tr, D_CONV), lambda i: (i, Z_GC)),
            pl.BlockSpec((CONV_WIDTH, D_CONV), lambda i: (0, 0)),
            vec, vec, vec,
            pl.BlockSpec((D_CONV, D_CONV), lambda i: (0, 0)),
        ],
        out_specs=[
            pl.BlockSpec((tr, D_CONV), lambda i: (i, 0)),
            pl.BlockSpec((CONV_WIDTH - 1, D_CONV), lambda i: (0, 0)),
        ],
        out_shape=[
            jax.ShapeDtypeStruct((s_len, D_CONV), bf16),
            jax.ShapeDtypeStruct((CONV_WIDTH - 1, D_CONV), f32),
        ],
        scratch_shapes=[
            pltpu.VMEM((tr + CONV_HALO, D_CONV), f32),
            pltpu.VMEM((tr, D_CONV), f32),
            pltpu.VMEM((SUBLANE - 1, tr + CONV_SHIFT_ROWS, D_CONV), f32),
        ],
        compiler_params=_cparams("arbitrary"),
        name="conv_prompt",
    )(z, z, dw_w, dw_b, lng, lnb, pw16)


def _conv_sample_kernel(st_ref, u_ref, g_ref, w_ref, b_ref, lng_ref, lnb_ref, pw_ref, o_ref, nst_ref):
    n_st = st_ref.shape[0]
    t_len = u_ref.shape[0]
    bsz = u_ref.shape[1]

    def ext(r):
        return st_ref[r] if r < n_st else u_ref[r - n_st]

    for t in range(t_len):
        acc = jnp.broadcast_to(b_ref[...], (bsz, D_CONV))
        for w in range(CONV_WIDTH):
            acc = acc + ext(t + w) * w_ref[w:w + 1, :]
        o_ref[t] = _ln_silu_pw(acc, lng_ref[...], lnb_ref[...], pw_ref[...]) * jax.nn.silu(g_ref[t])
    for r in range(n_st):
        nst_ref[r] = ext(r + t_len)


def _conv_sample(st_t, u_t, g_t, dw_w, dw_b, lng, lnb, pw16):
    t_len, bsz, _ = u_t.shape
    return pl.pallas_call(
        _conv_sample_kernel,
        out_shape=[
            jax.ShapeDtypeStruct((t_len, bsz, D_CONV), f32),
            jax.ShapeDtypeStruct(st_t.shape, f32),
        ],
        compiler_params=pltpu.CompilerParams(vmem_limit_bytes=VMEM_LIMIT),
        name="conv_sample",
    )(st_t, u_t, g_t, dw_w, dw_b, lng, lnb, pw16)


def _blockdiag(x16, w_ref):
    outs = [jnp.dot(x16[:, n * LRU_BLOCK:(n + 1) * LRU_BLOCK], w_ref[n], preferred_element_type=f32)
            for n in range(LRU_BLOCKS)]
    return jnp.concatenate(outs, axis=1)


def _lru_gates(xc, wa_ref, ba_ref, wx_ref, bx_ref, lam_ref):
    x16 = xc.astype(bf16)
    r = jax.nn.sigmoid(_blockdiag(x16, wa_ref) + ba_ref[...])
    i = jax.nn.sigmoid(_blockdiag(x16, wx_ref) + bx_ref[...])
    log_a = -LRU_C * r * jax.nn.softplus(-lam_ref[...])
    a = jnp.exp(log_a)
    one_minus_a2 = -jnp.tanh(log_a) * (a * a + 1.0)
    b = jnp.sqrt(one_minus_a2) * (i * xc)
    return a, b


LRU_HALO = 8


def _lru_prompt_kernel(x_ref, g_ref, cw_ref, cb_ref, wa_ref, ba_ref, wx_ref, bx_ref, lam_ref,
                       o_ref, st_ref, hl_ref, ext_ref, a_ref, b_ref, hs_ref, h_ref):
    i = pl.program_id(0)
    tr = x_ref.shape[0]
    pad = LRU_HALO - (LRU_CONV_WIDTH - 1)

    @pl.when(i == 0)
    def _():
        ext_ref[0:LRU_HALO, :] = jnp.zeros((LRU_HALO, D_LRU), f32)
        h_ref[...] = jnp.zeros_like(h_ref)

    ext_ref[LRU_HALO:LRU_HALO + tr, :] = x_ref[...]
    xc = jnp.broadcast_to(cb_ref[...], (tr, D_LRU))
    for w in range(LRU_CONV_WIDTH):
        xc = xc + ext_ref[pad + w:pad + w + tr, :] * cw_ref[w:w + 1, :]
    a, b = _lru_gates(xc, wa_ref, ba_ref, wx_ref, bx_ref, lam_ref)
    a_ref[...] = a
    b_ref[...] = b

    def body(t, h):
        h = a_ref[pl.ds(t, 1), :] * h + b_ref[pl.ds(t, 1), :]
        hs_ref[pl.ds(t, 1), :] = h
        return h

    h = lax.fori_loop(0, tr, body, h_ref[...], unroll=8)
    h_ref[...] = h
    hl_ref[...] = h
    o_ref[...] = (hs_ref[...] * jax.nn.silu(g_ref[...])).astype(bf16)
    tail = ext_ref[tr:tr + LRU_HALO, :]
    ext_ref[0:LRU_HALO, :] = tail
    st_ref[...] = ext_ref[pad:LRU_HALO, :]


def _lru_weight_specs(idx):
    vec = pl.BlockSpec((1, D_LRU), idx)
    blk = pl.BlockSpec((LRU_BLOCKS, LRU_BLOCK, LRU_BLOCK), lambda *a: (0, 0, 0))
    return [pl.BlockSpec((LRU_CONV_WIDTH, D_LRU), idx), vec, blk, vec, blk, vec, vec]


def _lru_prompt(z, lw):
    s_len = z.shape[0]
    tr = min(512, s_len)
    assert s_len % tr == 0
    return pl.pallas_call(
        _lru_prompt_kernel,
        grid=(s_len // tr,),
        in_specs=[pl.BlockSpec((tr, D_LRU), lambda i: (i, Z_LX)), pl.BlockSpec((tr, D_LRU), lambda i: (i, Z_GL))]
        + _lru_weight_specs(lambda i: (0, 0)),
        out_specs=[
            pl.BlockSpec((tr, D_LRU), lambda i: (i, 0)),
            pl.BlockSpec((LRU_CONV_WIDTH - 1, D_LRU), lambda i: (0, 0)),
            pl.BlockSpec((1, D_LRU), lambda i: (0, 0)),
        ],
        out_shape=[
            jax.ShapeDtypeStruct((s_len, D_LRU), bf16),
            jax.ShapeDtypeStruct((LRU_CONV_WIDTH - 1, D_LRU), f32),
            jax.ShapeDtypeStruct((1, D_LRU), f32),
        ],
        scratch_shapes=[
            pltpu.VMEM((tr + LRU_HALO, D_LRU), f32),
            pltpu.VMEM((tr, D_LRU), f32),
            pltpu.VMEM((tr, D_LRU), f32),
            pltpu.VMEM((tr, D_LRU), f32),
            pltpu.VMEM((1, D_LRU), f32),
        ],
        compiler_params=_cparams("arbitrary"),
        name="lru_prompt",
    )(z, z, *lw)


def _lru_sample_kernel(st_ref, x_ref, g_ref, h0_ref, cw_ref, cb_ref, wa_ref, ba_ref, wx_ref, bx_ref, lam_ref,
                       o_ref, nst_ref, hl_ref):
    n_st = st_ref.shape[0]
    t_len = x_ref.shape[0]
    bsz = x_ref.shape[1]

    def ext(r):
        return st_ref[r] if r < n_st else x_ref[r - n_st]

    h = h0_ref[...]
    for t in range(t_len):
        xc = jnp.broadcast_to(cb_ref[...], (bsz, D_LRU))
        for w in range(LRU_CONV_WIDTH):
            xc = xc + ext(t + w) * cw_ref[w:w + 1, :]
        a, b = _lru_gates(xc, wa_ref, ba_ref, wx_ref, bx_ref, lam_ref)
        h = a * h + b
        o_ref[t] = h * jax.nn.silu(g_ref[t])
    hl_ref[...] = h
    for r in range(n_st):
        nst_ref[r] = ext(r + t_len)


def _lru_sample(st_t, x_t, g_t, h0, lw):
    t_len, bsz, _ = x_t.shape
    return pl.pallas_call(
        _lru_sample_kernel,
        out_shape=[
            jax.ShapeDtypeStruct((t_len, bsz, D_LRU), f32),
            jax.ShapeDtypeStruct(st_t.shape, f32),
            jax.ShapeDtypeStruct((bsz, D_LRU), f32),
        ],
        compiler_params=pltpu.CompilerParams(vmem_limit_bytes=VMEM_LIMIT),
        name="lru_sample",
    )(st_t, x_t, g_t, h0, *lw)


MERGE_TM = 1024
MERGE_TN = 1024


def _merge_kernel(ya_ref, yc_ref, yl_ref, x_ref, w_ref, o_ref):
    acc = jnp.dot(ya_ref[...], w_ref[0:D_ATTN, :], preferred_element_type=f32)
    acc = acc + jnp.dot(yc_ref[...], w_ref[D_ATTN:D_ATTN + D_CONV, :], preferred_element_type=f32)
    acc = acc + jnp.dot(yl_ref[...], w_ref[D_ATTN + D_CONV:, :], preferred_element_type=f32)
    o_ref[...] = x_ref[...] + acc


def _merge(x2d, y_attn, y_conv, y_lru, w_out16):
    n = x2d.shape[0]
    tm = min(MERGE_TM, n)
    assert n % tm == 0
    return pl.pallas_call(
        _merge_kernel,
        grid=(n // tm, D_MODEL // MERGE_TN),
        in_specs=[
            pl.BlockSpec((tm, D_ATTN), lambda i, j: (i, 0)),
            pl.BlockSpec((tm, D_CONV), lambda i, j: (i, 0)),
            pl.BlockSpec((tm, D_LRU), lambda i, j: (i, 0)),
            pl.BlockSpec((tm, MERGE_TN), lambda i, j: (i, j)),
            pl.BlockSpec((D_MODEL, MERGE_TN), lambda i, j: (0, j)),
        ],
        out_specs=pl.BlockSpec((tm, MERGE_TN), lambda i, j: (i, j)),
        out_shape=jax.ShapeDtypeStruct((n, D_MODEL), f32),
        compiler_params=_cparams("arbitrary", "arbitrary"),
        name="merge",
    )(y_attn, y_conv, y_lru, x2d, w_out16)


W_IN_SEGMENTS = (("q", D_ATTN), ("k", D_ATTN), ("v", D_ATTN), ("f", N_HEADS), ("ga", D_ATTN),
                 ("glu_a", D_CONV), ("glu_b", D_CONV), ("gc", D_CONV), ("lx", D_LRU), ("gl", D_LRU))
W_MAIN_ORDER = ("q", "k", "v", "ga", "gc", "lx", "gl", "glu_b", "glu_a")
D_IN = sum(size for _, size in W_IN_SEGMENTS)
PREP_TR = 256


def _prep_w_in_kernel(wt_ref, wm_ref, wf_ref):
    src = {}
    o = 0
    for name, size in W_IN_SEGMENTS:
        src[name] = (o, size)
        o += size
    d = 0
    for name in W_MAIN_ORDER:
        o, size = src[name]
        wm_ref[:, d:d + size] = wt_ref[0, o:o + size, :].T.astype(bf16)
        d += size
    o, size = src["f"]
    wf_ref[...] = jnp.zeros_like(wf_ref)
    wf_ref[:, 0:size] = wt_ref[0, o:o + size, :].T.astype(bf16)


def _prep_w_in(w_in_t, layer):
    assert all(sum(size for _, size in W_IN_SEGMENTS[:n]) % SUBLANE == 0 for n in range(len(W_IN_SEGMENTS)))
    return pl.pallas_call(
        _prep_w_in_kernel,
        grid=(D_MODEL // PREP_TR,),
        in_specs=[pl.BlockSpec((1, D_IN, PREP_TR), lambda i: (layer, 0, i))],
        out_specs=[pl.BlockSpec((PREP_TR, W_COLS), lambda i: (i, 0)), pl.BlockSpec((PREP_TR, LANE), lambda i: (i, 0))],
        out_shape=[jax.ShapeDtypeStruct((D_MODEL, W_COLS), bf16), jax.ShapeDtypeStruct((D_MODEL, LANE), bf16)],
        compiler_params=_cparams("arbitrary"),
        name="prep_w_in",
    )(w_in_t)


def kernel(x_prompt, x_sample, cache_k, cache_v, cache_logf, state_conv, state_lru_conv, state_lru_h,
           page_table, norm_g, w_in, b_f, q_norm_g, k_norm_g, conv_dw_w, conv_dw_b, conv_ln_g, conv_ln_b,
           conv_pw_w, lru_conv_w, lru_conv_b, lru_w_a, lru_b_a, lru_w_x, lru_b_x, lru_lambda, w_out):
    depth = w_in.shape[0]
    n_seq_p, s_len, _ = x_prompt.shape
    bsz, t_len, _ = x_sample.shape
    assert n_seq_p == 1, "the prompt kernels carry one sequence through their row tiles"
    n_pool = cache_k.shape[1]
    cache_k4 = cache_k.reshape(depth, n_pool, PAGE_SIZE * N_HEADS, HEAD_DIM)
    cache_v4 = cache_v.reshape(depth, n_pool, PAGE_SIZE * N_HEADS, HEAD_DIM)
    cache_lft = jnp.swapaxes(cache_logf, 2, 3)
    w_in_t = jnp.swapaxes(w_in, 1, 2)

    xp = x_prompt.reshape(s_len, D_MODEL)
    xs = x_sample.reshape(bsz * t_len, D_MODEL)
    outs = {k: [] for k in ("pk", "pv", "pf", "pc", "plc", "ph", "sk", "sv", "sf", "sc", "slc", "sh")}
    row = lambda v: v.reshape(1, -1)
    for l in range(depth):
        wm, wf = _prep_w_in(w_in_t, l)
        bfp = jnp.pad(b_f[l], (0, LANE - N_HEADS)).reshape(1, LANE)
        proj_w = (row(norm_g[l]), wm, wf, bfp, row(q_norm_g[l]), row(k_norm_g[l]))
        conv_w = (conv_dw_w[l], row(conv_dw_b[l]), row(conv_ln_g[l]), row(conv_ln_b[l]), conv_pw_w[l].astype(bf16))
        lru_w = (lru_conv_w[l], row(lru_conv_b[l]), lru_w_a[l].astype(bf16), row(lru_b_a[l]),
                 lru_w_x[l].astype(bf16), row(lru_b_x[l]), row(lru_lambda[l]))
        w_out16 = w_out[l].astype(bf16)

        z, k_p, v_p, qkv16, logf, cs = _project(xp, *proj_w, seq_rows=s_len, carry_over=True)
        y_att = _attn_prompt(qkv16, cs[:, :N_HEADS].T.reshape(N_HEADS, 1, s_len), cs, z)
        y_conv, st_c = _conv_prompt(z, *conv_w)
        y_lru, st_l, h_l = _lru_prompt(z, lru_w)
        xp = _merge(xp, y_att, y_conv, y_lru, w_out16)
        outs["pk"].append(k_p.reshape(1, s_len, N_HEADS, HEAD_DIM))
        outs["pv"].append(v_p.reshape(1, s_len, N_HEADS, HEAD_DIM))
        outs["pf"].append(logf[:, :N_HEADS].reshape(1, s_len, N_HEADS))
        outs["pc"].append(st_c[None])
        outs["plc"].append(st_l[None])
        outs["ph"].append(h_l)

        z, k_s, v_s, _, logf, cs = _project(xs, *proj_w, seq_rows=t_len, carry_over=False)
        z3 = z.reshape(bsz, t_len, Z_COLS)
        k3 = k_s.reshape(bsz, t_len, D_ATTN)
        v3 = v_s.reshape(bsz, t_len, D_ATTN)
        cs3 = cs[:, :N_HEADS].reshape(bsz, t_len, N_HEADS)
        cs_ht = jnp.swapaxes(cs3, 1, 2)
        cscol = jnp.pad(cs_ht, ((0, 0), (0, 0), (0, SUBLANE - t_len))).reshape(bsz, N_HEADS * SUBLANE, 1)
        csn = jnp.repeat(cs_ht, SUBLANE, axis=1)
        y_att = _attn_sample(z3, k3, v3, cscol, csn, cache_k4, cache_v4, cache_lft, page_table, l)
        tmaj = lambda a: jnp.swapaxes(a, 0, 1)
        zcol_t = lambda c: tmaj(z3[:, :, c * PROJ_TN:(c + 1) * PROJ_TN])
        bmaj16 = lambda a_t: tmaj(a_t).reshape(bsz * t_len, -1).astype(bf16)
        y_conv_t, nst_c = _conv_sample(tmaj(state_conv[l]), zcol_t(Z_U), zcol_t(Z_GC), *conv_w)
        y_lru_t, nst_l, h_l = _lru_sample(tmaj(state_lru_conv[l]), zcol_t(Z_LX), zcol_t(Z_GL), state_lru_h[l], lru_w)
        xs = _merge(xs, y_att.reshape(bsz * t_len, D_ATTN).astype(bf16), bmaj16(y_conv_t), bmaj16(y_lru_t), w_out16)
        outs["sk"].append(k3.reshape(bsz, t_len, N_HEADS, HEAD_DIM))
        outs["sv"].append(v3.reshape(bsz, t_len, N_HEADS, HEAD_DIM))
        outs["sf"].append(logf[:, :N_HEADS].reshape(bsz, t_len, N_HEADS))
        outs["sc"].append(tmaj(nst_c))
        outs["slc"].append(tmaj(nst_l))
        outs["sh"].append(h_l)

    st = {k: jnp.stack(v) for k, v in outs.items()}
    return (xp.reshape(1, s_len, D_MODEL), xs.reshape(bsz, t_len, D_MODEL),
            st["pk"], st["pv"], st["pf"], st["pc"], st["plc"], st["ph"],
            st["sk"], st["sv"], st["sf"], st["sc"], st["slc"], st["sh"])
```

```python
import functools
import math

import jax
import jax.numpy as jnp
from jax import lax
from jax.experimental import pallas as pl
from jax.experimental.pallas import tpu as pltpu

f32 = jnp.float32
bf16 = jnp.bfloat16

D_MODEL = 2048
HEAD_DIM = 128
N_HEADS = 8
D_ATTN = N_HEADS * HEAD_DIM
D_CONV = 512
CONV_WIDTH = 31
D_LRU = 512
LRU_BLOCKS = 4
LRU_BLOCK = D_LRU // LRU_BLOCKS
LRU_CONV_WIDTH = 4
LRU_C = 8.0
PAGE_SIZE = 128
EPS = 1e-6
SCALE = HEAD_DIM ** -0.5
LOG2E = math.log2(math.e)

LANE = 128
SUBLANE = 8
VMEM_LIMIT = 48 * 1024 * 1024

PROJ_TN = 512
PROJ_TM = 1024
CUMSUM_ROWS = 256
PROJ_SPLIT = 2
W_Q, W_K, W_V, W_GA, W_GC, W_LX, W_GL, W_SIG, W_U = 0, 2, 4, 6, 8, 9, 10, 11, 12
W_COLS = 13 * PROJ_TN
Z_Q, Z_GA, Z_GC, Z_LX, Z_GL, Z_SIG, Z_U = 0, 2, 4, 5, 6, 7, 8
Z_COLS = 9 * PROJ_TN
Z_SHIFT = W_GA - Z_GA


def _cparams(*sem):
    return pltpu.CompilerParams(dimension_semantics=sem, vmem_limit_bytes=VMEM_LIMIT)


def _split3(x):
    hi = x.astype(bf16).astype(f32)
    r1 = x - hi
    mid = r1.astype(bf16).astype(f32)
    lo = (r1 - mid).astype(bf16).astype(f32)
    return hi, mid, lo


def _head_rmsnorm(z, g):
    outs = []
    for h in range(PROJ_TN // HEAD_DIM):
        zh = z[:, h * HEAD_DIM:(h + 1) * HEAD_DIM]
        ms = jnp.mean(zh * zh, axis=-1, keepdims=True)
        outs.append(zh * lax.rsqrt(ms + EPS) * g)
    return jnp.concatenate(outs, axis=1)


def _proj_kernel(x_ref, g_ref, wm_ref, wf_ref, bf_ref, qg_ref, kg_ref,
                 z_ref, k_ref, v_ref, qkv_ref, logf_ref, cs_ref,
                 xn_ref, sig_ref, carry_ref, *, seq_rows, carry_over):
    i = pl.program_id(0)
    j = pl.program_id(1)
    tm = x_ref.shape[0]

    @pl.when(j == 0)
    def _():
        tb = min(tm, CUMSUM_ROWS)
        ii = lax.broadcasted_iota(jnp.int32, (tb, tb), 0)
        jj = lax.broadcasted_iota(jnp.int32, (tb, tb), 1)
        keep = jj <= ii
        if not carry_over:
            keep = jnp.logical_and(keep, (ii // seq_rows) == (jj // seq_rows))
        tri = jnp.where(keep, 1.0, 0.0).astype(bf16)
        if carry_over:
            @pl.when(i == 0)
            def _():
                carry_ref[...] = jnp.zeros_like(carry_ref)
        for r0 in range(0, tm, tb):
            x = x_ref[r0:r0 + tb, :]
            y = x * lax.rsqrt(jnp.mean(x * x, axis=-1, keepdims=True) + EPS) * g_ref[...]
            xn = y.astype(bf16)
            xn_ref[r0:r0 + tb, :] = xn
            f = jnp.dot(xn, wf_ref[...], preferred_element_type=f32) + bf_ref[...]
            logf = jax.nn.log_sigmoid(f)
            logf_ref[r0:r0 + tb, :] = logf
            parts = jnp.concatenate(_split3(logf), axis=1).astype(bf16)
            c3 = jnp.dot(tri, parts, preferred_element_type=f32)
            cs = c3[:, :LANE] + c3[:, LANE:2 * LANE] + c3[:, 2 * LANE:]
            if carry_over:
                cs = cs + carry_ref[...]
                carry_ref[...] = cs[tb - 1:tb, :]
            cs_ref[r0:r0 + tb, :] = cs

    halves = [(r0, tm // PROJ_SPLIT) for r0 in range(0, tm, tm // PROJ_SPLIT)]

    def tiles():
        return [(slice(r0, r0 + n), jnp.dot(xn_ref[r0:r0 + n, :], wm_ref[...], preferred_element_type=f32))
                for r0, n in halves]

    @pl.when(j < W_K)
    def _():
        for rows, z in tiles():
            qn = _head_rmsnorm(z, qg_ref[...])
            z_ref[rows, :] = qn
            qkv_ref[rows, :] = (qn * (SCALE * LOG2E)).astype(bf16)

    @pl.when(jnp.logical_and(j >= W_K, j < W_V))
    def _():
        for rows, z in tiles():
            kn = _head_rmsnorm(z, kg_ref[...])
            k_ref[rows, :] = kn
            qkv_ref[rows, :] = kn.astype(bf16)

    @pl.when(jnp.logical_and(j >= W_V, j < W_GA))
    def _():
        for rows, z in tiles():
            v_ref[rows, :] = z
            qkv_ref[rows, :] = z.astype(bf16)

    @pl.when(jnp.logical_and(j >= W_GA, j < W_SIG))
    def _():
        for rows, z in tiles():
            z_ref[rows, :] = z

    @pl.when(j == W_SIG)
    def _():
        for rows, z in tiles():
            s = jax.nn.sigmoid(z)
            sig_ref[rows, :] = s
            z_ref[rows, :] = s

    @pl.when(j == W_U)
    def _():
        for rows, z in tiles():
            z_ref[rows, :] = z * sig_ref[rows, :]


def _project(x2d, norm_g, wm, wf, bfp, qg, kg, *, seq_rows, carry_over):
    n = x2d.shape[0]
    tm = min(PROJ_TM, n)
    tb = min(tm, CUMSUM_ROWS)
    assert n % tm == 0 and tm % tb == 0 and (carry_over or tb % seq_rows == 0)
    grid = (n // tm, W_COLS // PROJ_TN)
    clamp = lambda j, lo, hi: jnp.minimum(jnp.maximum(j, lo), hi) - lo
    kern = functools.partial(_proj_kernel, seq_rows=seq_rows, carry_over=carry_over)
    return pl.pallas_call(
        kern,
        grid=grid,
        in_specs=[
            pl.BlockSpec((tm, D_MODEL), lambda i, j: (i, 0)),
            pl.BlockSpec((1, D_MODEL), lambda i, j: (0, 0)),
            pl.BlockSpec((D_MODEL, PROJ_TN), lambda i, j: (0, j)),
            pl.BlockSpec((D_MODEL, LANE), lambda i, j: (0, 0)),
            pl.BlockSpec((1, LANE), lambda i, j: (0, 0)),
            pl.BlockSpec((1, HEAD_DIM), lambda i, j: (0, 0)),
            pl.BlockSpec((1, HEAD_DIM), lambda i, j: (0, 0)),
        ],
        out_specs=[
            pl.BlockSpec((tm, PROJ_TN), lambda i, j: (i, jnp.where(j < W_K, j, jnp.maximum(j - Z_SHIFT, Z_GA)))),
            pl.BlockSpec((tm, PROJ_TN), lambda i, j: (i, clamp(j, W_K, W_V - 1))),
            pl.BlockSpec((tm, PROJ_TN), lambda i, j: (i, clamp(j, W_V, W_GA - 1))),
            pl.BlockSpec((tm, PROJ_TN), lambda i, j: (i, clamp(j, W_Q, W_GA - 1))),
            pl.BlockSpec((tm, LANE), lambda i, j: (i, 0)),
            pl.BlockSpec((tm, LANE), lambda i, j: (i, 0)),
        ],
        out_shape=[
            jax.ShapeDtypeStruct((n, Z_COLS), f32),
            jax.ShapeDtypeStruct((n, D_ATTN), f32),
            jax.ShapeDtypeStruct((n, D_ATTN), f32),
            jax.ShapeDtypeStruct((n, 3 * D_ATTN), bf16),
            jax.ShapeDtypeStruct((n, LANE), f32),
            jax.ShapeDtypeStruct((n, LANE), f32),
        ],
        scratch_shapes=[
            pltpu.VMEM((tm, D_MODEL), bf16),
            pltpu.VMEM((tm, PROJ_TN), f32),
            pltpu.VMEM((1, LANE), f32),
        ],
        compiler_params=_cparams("arbitrary", "arbitrary"),
        name="proj",
    )(x2d, norm_g, wm, wf, bfp, qg, kg)


ATTN_TQ = 1024


def _attn_prompt_kernel(q_ref, qn_ref, k_ref, v_ref, crow_ref, cs_ref, g_ref, o_ref, s_ref, p_ref, m_ref, acc_ref):
    qi = pl.program_id(1)
    tq = q_ref.shape[0]
    q = q_ref[...]
    lane = lax.broadcasted_iota(jnp.int32, cs_ref.shape, 1)
    cq = jnp.sum(jnp.where(lane == pl.program_id(0), cs_ref[...], 0.0), axis=-1, keepdims=True) * LOG2E

    def chunk(kj):
        return pl.ds(pl.multiple_of(kj * tq, tq), tq)

    def qk_into_s(kj, q=q):
        s_ref[...] = lax.dot_general(q, k_ref[chunk(kj), :], (((1,), (1,)), ((), ())),
                                     preferred_element_type=f32)

    ones = jnp.ones((tq, HEAD_DIM), bf16)

    def pv(kj):
        v_aug = jnp.concatenate([v_ref[chunk(kj), :], ones], axis=1)
        return jnp.dot(p_ref[...], v_aug, preferred_element_type=f32)

    def softmax_of_s(kj, masked):
        s = s_ref[...] - crow_ref[0, :, chunk(kj)] * LOG2E
        if masked:
            row = lax.broadcasted_iota(jnp.int32, (tq, tq), 0)
            col = lax.broadcasted_iota(jnp.int32, (tq, tq), 1)
            s = jnp.where(col <= row, s, -jnp.inf)
        m = m_ref[...]
        m_new = jnp.maximum(m, jnp.max(s, axis=-1, keepdims=True))
        shift = (m_new + cq) - cq
        alpha = jnp.exp2(m - shift)
        p = jnp.exp2(s - shift)
        m_ref[...] = shift
        return p.astype(bf16), alpha

    m_ref[...] = jnp.full_like(m_ref, -jnp.inf)
    acc_ref[...] = jnp.zeros_like(acc_ref)
    p_ref[...] = jnp.zeros_like(p_ref)

    @pl.when(qi == 0)
    def _():
        qk_into_s(0)

    def body(j, _):
        pv_prev = pv(jnp.maximum(j - 1, 0))
        p, alpha = softmax_of_s(j, False)
        qk_into_s(j + 1)
        p_ref[...] = p
        acc_ref[...] = alpha * (pv_prev + acc_ref[...])
        return 0

    lax.fori_loop(0, qi, body, 0)
    pv_prev = pv(jnp.maximum(qi - 1, 0))
    p, alpha = softmax_of_s(qi, True)
    qk_into_s(0, qn_ref[...])
    p_ref[...] = p
    acc = pv(qi) + alpha * (pv_prev + acc_ref[...])
    o_ref[...] = ((acc[:, :HEAD_DIM] / acc[:, HEAD_DIM:]) * jax.nn.silu(g_ref[...])).astype(bf16)


def _attn_prompt(qkv16, crow, cs, z):
    s_len = qkv16.shape[0]
    tq = min(ATTN_TQ, s_len)
    assert s_len % tq == 0
    n_q = s_len // tq
    return pl.pallas_call(
        _attn_prompt_kernel,
        grid=(N_HEADS, n_q),
        scratch_shapes=[
            pltpu.VMEM((tq, tq), f32),
            pltpu.VMEM((tq, tq), bf16),
            pltpu.VMEM((tq, 1), f32),
            pltpu.VMEM((tq, 2 * HEAD_DIM), f32),
        ],
        in_specs=[
            pl.BlockSpec((tq, HEAD_DIM), lambda h, i: (i, h)),
            pl.BlockSpec((tq, HEAD_DIM), lambda h, i: (jnp.minimum(i + 1, n_q - 1), h)),
            pl.BlockSpec((s_len, HEAD_DIM), lambda h, i: (0, N_HEADS + h)),
            pl.BlockSpec((s_len, HEAD_DIM), lambda h, i: (0, 2 * N_HEADS + h)),
            pl.BlockSpec((1, 1, s_len), lambda h, i: (h, 0, 0)),
            pl.BlockSpec((tq, LANE), lambda h, i: (i, 0)),
            pl.BlockSpec((tq, HEAD_DIM), lambda h, i: (i, Z_GA * PROJ_TN // HEAD_DIM + h)),
        ],
        out_specs=pl.BlockSpec((tq, HEAD_DIM), lambda h, i: (i, h)),
        out_shape=jax.ShapeDtypeStruct((s_len, D_ATTN), bf16),
        compiler_params=_cparams("arbitrary", "arbitrary"),
        name="attn_prompt",
    )(qkv16, qkv16, qkv16, qkv16, crow, cs, z)


def _decode_kernel(pt_ref, z_q_ref, z_g_ref, kn_ref, vn_ref, cscol_ref, csn_ref, *rest, pages_per_step, t_len):
    pp = pages_per_step
    k_refs = rest[:pp]
    v_refs = rest[pp:2 * pp]
    lf_refs = rest[2 * pp:3 * pp]
    o_ref = rest[3 * pp]
    qp_ref, m_ref, l_ref, acc_ref, carry_ref = rest[3 * pp + 1:]
    g = pl.program_id(1)
    n_rows = N_HEADS * SUBLANE
    n_pairs = N_HEADS // 2
    hsl = lambda h: slice(h * HEAD_DIM, (h + 1) * HEAD_DIM)
    rsl = lambda h: slice(h * SUBLANE, (h + 1) * SUBLANE)
    psl = lambda i: slice(2 * i * SUBLANE, 2 * (i + 1) * SUBLANE)
    half = lambda h: slice((h % 2) * HEAD_DIM, (h % 2 + 1) * HEAD_DIM)

    @pl.when(g == 0)
    def _():
        qp_ref[...] = jnp.zeros_like(qp_ref)
        q = z_q_ref[0] * SCALE
        for h in range(N_HEADS):
            qp_ref[h * SUBLANE:h * SUBLANE + t_len, half(h)] = q[:, hsl(h)]
        m_ref[...] = jnp.full_like(m_ref, -jnp.inf)
        l_ref[...] = jnp.zeros_like(l_ref)
        acc_ref[...] = jnp.zeros_like(acc_ref)
        carry_ref[...] = jnp.zeros_like(carry_ref)

    ii = lax.broadcasted_iota(jnp.int32, (PAGE_SIZE, PAGE_SIZE), 0)
    jj = lax.broadcasted_iota(jnp.int32, (PAGE_SIZE, PAGE_SIZE), 1)
    upper = jnp.where(ii <= jj, 1.0, 0.0).astype(bf16)
    q16 = qp_ref[...].astype(bf16)

    head_rows = lambda ref, h: ref[0, 0, pl.ds(h, PAGE_SIZE, stride=N_HEADS), :].astype(bf16)
    c_pages = []
    carry = carry_ref[...]
    for pg in range(pp):
        parts = jnp.concatenate(_split3(lf_refs[pg][0, 0]), axis=0).astype(bf16)
        c3 = jnp.dot(parts, upper, preferred_element_type=f32)
        c_page = (c3[:N_HEADS] + c3[N_HEADS:2 * N_HEADS] + c3[2 * N_HEADS:]) + carry
        carry = jnp.broadcast_to(c_page[:, PAGE_SIZE - 1:PAGE_SIZE], (N_HEADS, LANE))
        c_pages.append(c_page)
    carry_ref[...] = carry
    pair_rows = lambda ref, i: jnp.concatenate([head_rows(ref, 2 * i), head_rows(ref, 2 * i + 1)], axis=1)
    s = jnp.concatenate(
        [jnp.concatenate(
            [lax.dot_general(q16[psl(i)], pair_rows(k_refs[pg], i), (((1,), (1,)), ((), ())),
                             preferred_element_type=f32)
             for i in range(n_pairs)], axis=0) - jnp.repeat(c_pages[pg], SUBLANE, axis=0)
         for pg in range(pp)], axis=1)
    m = m_ref[...]
    m_new = jnp.maximum(m, jnp.max(s, axis=-1, keepdims=True))
    alpha = jnp.exp(m - m_new)
    p = jnp.exp(s - m_new)
    l_ref[...] = alpha * l_ref[...] + jnp.sum(p, axis=-1, keepdims=True)
    p16 = p.astype(bf16)
    pv = []
    for i in range(n_pairs):
        pv_i = jnp.zeros((2 * SUBLANE, 2 * HEAD_DIM), f32)
        for pg in range(pp):
            pv_i = pv_i + jnp.dot(p16[psl(i), pg * PAGE_SIZE:(pg + 1) * PAGE_SIZE], pair_rows(v_refs[pg], i),
                                  preferred_element_type=f32)
        pv += [pv_i[:SUBLANE, :HEAD_DIM], pv_i[SUBLANE:, HEAD_DIM:]]
    acc_ref[...] = alpha * acc_ref[...] + jnp.concatenate(pv, axis=0)
    m_ref[...] = m_new

    @pl.when(g == pl.num_programs(1) - 1)
    def _():
        c_last = jnp.concatenate(
            [jnp.broadcast_to(carry_ref[h:h + 1, 0:1], (SUBLANE, 1)) for h in range(N_HEADS)], axis=0)
        ct_row = c_last + cscol_ref[0]
        ct_new = c_last + csn_ref[0]
        qp = qp_ref[...]
        kn = kn_ref[0]
        vn = vn_ref[0]
        s_n = jnp.concatenate(
            [jnp.concatenate([jnp.sum(qp[rsl(h), half(h)] * kn[t:t + 1, hsl(h)], axis=-1, keepdims=True)
                              for t in range(t_len)], axis=1)
             for h in range(N_HEADS)], axis=0) - ct_new
        t_row = lax.broadcasted_iota(jnp.int32, (n_rows, t_len), 0) % SUBLANE
        t_col = lax.broadcasted_iota(jnp.int32, (n_rows, t_len), 1)
        s_n = jnp.where(t_col <= t_row, s_n, -jnp.inf)
        m = m_ref[...]
        m_new = jnp.maximum(m, jnp.max(s_n, axis=-1, keepdims=True))
        shift = (m_new + ct_row) - ct_row
        alpha = jnp.exp(m - shift)
        p = jnp.exp(s_n - shift)
        l = alpha * l_ref[...] + jnp.sum(p, axis=-1, keepdims=True)
        acc = alpha * acc_ref[...]
        for h in range(N_HEADS):
            acc_h = acc[rsl(h)]
            for t in range(t_len):
                acc_h = acc_h + p[rsl(h), t:t + 1] * vn[t:t + 1, hsl(h)]
            o_ref[0, :, hsl(h)] = (acc_h / l[rsl(h)])[:t_len] * jax.nn.silu(z_g_ref[0, :, hsl(h)])


DECODE_PAGES_PER_STEP = 16


def _attn_sample(z3, k3, v3, cscol, csn, cache_k4, cache_v4, cache_lft, page_table, layer):
    bsz, t_len, _ = z3.shape
    n_pages = page_table.shape[1]
    pp = min(DECODE_PAGES_PER_STEP, n_pages)
    assert n_pages % pp == 0 and t_len <= SUBLANE
    n_rows = N_HEADS * SUBLANE

    def page_spec(shape, pg):
        return pl.BlockSpec(shape, lambda b, g, pt: (layer, pt[b, g * pp + pg], 0, 0))

    z_blk = lambda c: pl.BlockSpec((1, t_len, D_ATTN), lambda b, g, pt: (b, 0, c))
    in_specs = [z_blk(Z_Q * PROJ_TN // D_ATTN), z_blk(Z_GA * PROJ_TN // D_ATTN), z_blk(0), z_blk(0),
                pl.BlockSpec((1, n_rows, 1), lambda b, g, pt: (b, 0, 0)),
                pl.BlockSpec((1, n_rows, t_len), lambda b, g, pt: (b, 0, 0))]
    in_specs += [page_spec((1, 1, PAGE_SIZE * N_HEADS, HEAD_DIM), pg) for pg in range(pp)]
    in_specs += [page_spec((1, 1, PAGE_SIZE * N_HEADS, HEAD_DIM), pg) for pg in range(pp)]
    in_specs += [page_spec((1, 1, N_HEADS, PAGE_SIZE), pg) for pg in range(pp)]
    grid_spec = pltpu.PrefetchScalarGridSpec(
        num_scalar_prefetch=1,
        grid=(bsz, n_pages // pp),
        in_specs=in_specs,
        out_specs=pl.BlockSpec((1, t_len, D_ATTN), lambda b, g, pt: (b, 0, 0)),
        scratch_shapes=[
            pltpu.VMEM((n_rows, 2 * HEAD_DIM), f32),
            pltpu.VMEM((n_rows, 1), f32),
            pltpu.VMEM((n_rows, 1), f32),
            pltpu.VMEM((n_rows, HEAD_DIM), f32),
            pltpu.VMEM((N_HEADS, LANE), f32),
        ],
    )
    return pl.pallas_call(
        functools.partial(_decode_kernel, pages_per_step=pp, t_len=t_len),
        grid_spec=grid_spec,
        out_shape=jax.ShapeDtypeStruct((bsz, t_len, D_ATTN), f32),
        compiler_params=_cparams("arbitrary", "arbitrary"),
        name="attn_sample",
    )(page_table, z3, z3, k3, v3, cscol, csn,
      *([cache_k4] * pp), *([cache_v4] * pp), *([cache_lft] * pp))


def _ln_silu_pw(y, lng, lnb, pw16):
    mu = jnp.mean(y, axis=-1, keepdims=True)
    yc = y - mu
    yn = yc * lax.rsqrt(jnp.mean(yc * yc, axis=-1, keepdims=True) + EPS) * lng + lnb
    return jnp.dot(jax.nn.silu(yn).astype(bf16), pw16, preferred_element_type=f32)


CONV_HALO = 32
CONV_CHUNK = 64
CONV_SHIFT_ROWS = CONV_HALO - SUBLANE


def _conv_prompt_kernel(u_ref, g_ref, w_ref, b_ref, lng_ref, lnb_ref, pw_ref, o_ref, st_ref, ext_ref, y_ref, sh_ref):
    i = pl.program_id(0)
    tr = u_ref.shape[0]
    pad = CONV_HALO - (CONV_WIDTH - 1)

    @pl.when(i == 0)
    def _():
        ext_ref[0:CONV_HALO, :] = jnp.zeros((CONV_HALO, D_CONV), f32)

    ext_ref[CONV_HALO:CONV_HALO + tr, :] = u_ref[...]
    for r in range(1, SUBLANE):
        sh_ref[r - 1] = ext_ref[r:r + tr + CONV_SHIFT_ROWS, :]
    for c in range(tr // CONV_CHUNK):
        r0 = c * CONV_CHUNK
        acc = jnp.broadcast_to(b_ref[...], (CONV_CHUNK, D_CONV))
        for w in range(CONV_WIDTH):
            a, r = divmod(pad + w, SUBLANE)
            lo = r0 + a * SUBLANE
            rows = ext_ref[lo:lo + CONV_CHUNK, :] if r == 0 else sh_ref[r - 1, lo:lo + CONV_CHUNK, :]
            acc = acc + rows * w_ref[w:w + 1, :]
        y_ref[r0:r0 + CONV_CHUNK, :] = acc
    o = _ln_silu_pw(y_ref[...], lng_ref[...], lnb_ref[...], pw_ref[...])
    o_ref[...] = (o * jax.nn.silu(g_ref[...])).astype(bf16)
    tail = ext_ref[tr:tr + CONV_HALO, :]
    ext_ref[0:CONV_HALO, :] = tail
    st_ref[...] = ext_ref[pad:CONV_HALO, :]


def _conv_prompt(z, dw_w, dw_b, lng, lnb, pw16):
    s_len = z.shape[0]
    tr = min(512, s_len)
    assert s_len % tr == 0 and tr % CONV_CHUNK == 0
    vec = pl.BlockSpec((1, D_CONV), lambda i: (0, 0))
    return pl.pallas_call(
        _conv_prompt_kernel,
        grid=(s_len // tr,),
        in_specs=[
            pl.BlockSpec((tr, D_CONV), lambda i: (i, Z_U)),
            pl.BlockSpec((tr, D_CONV), lambda i: (i, Z_GC)),
            pl.BlockSpec((CONV_WIDTH, D_CONV), lambda i: (0, 0)),
            vec, vec, vec,
            pl.BlockSpec((D_CONV, D_CONV), lambda i: (0, 0)),
        ],
        out_specs=[
            pl.BlockSpec((tr, D_CONV), lambda i: (i, 0)),
            pl.BlockSpec((CONV_WIDTH - 1, D_CONV), lambda i: (0, 0)),
        ],
        out_shape=[
            jax.ShapeDtypeStruct((s_len, D_CONV), bf16),
            jax.ShapeDtypeStruct((CONV_WIDTH - 1, D_CONV), f32),
        ],
        scratch_shapes=[
            pltpu.VMEM((tr + CONV_HALO, D_CONV), f32),
            pltpu.VMEM((tr, D_CONV), f32),
            pltpu.VMEM((SUBLANE - 1, tr + CONV_SHIFT_ROWS, D_CONV), f32),
        ],
        compiler_params=_cparams("arbitrary"),
        name="conv_prompt",
    )(z, z, dw_w, dw_b, lng, lnb, pw16)


def _conv_sample_kernel(st_ref, u_ref, g_ref, w_ref, b_ref, lng_ref, lnb_ref, pw_ref, o_ref, nst_ref):
    n_st = st_ref.shape[0]
    t_len = u_ref.shape[0]
    bsz = u_ref.shape[1]

    def ext(r):
        return st_ref[r] if r < n_st else u_ref[r - n_st]

    for t in range(t_len):
        acc = jnp.broadcast_to(b_ref[...], (bsz, D_CONV))
        for w in range(CONV_WIDTH):
            acc = acc + ext(t + w) * w_ref[w:w + 1, :]
        o_ref[t] = _ln_silu_pw(acc, lng_ref[...], lnb_ref[...], pw_ref[...]) * jax.nn.silu(g_ref[t])
    for r in range(n_st):
        nst_ref[r] = ext(r + t_len)


def _conv_sample(st_t, u_t, g_t, dw_w, dw_b, lng, lnb, pw16):
    t_len, bsz, _ = u_t.shape
    return pl.pallas_call(
        _conv_sample_kernel,
        out_shape=[
            jax.ShapeDtypeStruct((t_len, bsz, D_CONV), f32),
            jax.ShapeDtypeStruct(st_t.shape, f32),
        ],
        compiler_params=pltpu.CompilerParams(vmem_limit_bytes=VMEM_LIMIT),
        name="conv_sample",
    )(st_t, u_t, g_t, dw_w, dw_b, lng, lnb, pw16)


def _blockdiag(x16, w_ref):
    outs = [jnp.dot(x16[:, n * LRU_BLOCK:(n + 1) * LRU_BLOCK], w_ref[n], preferred_element_type=f32)
            for n in range(LRU_BLOCKS)]
    return jnp.concatenate(outs, axis=1)


def _lru_gates(xc, wa_ref, ba_ref, wx_ref, bx_ref, lam_ref):
    x16 = xc.astype(bf16)
    r = jax.nn.sigmoid(_blockdiag(x16, wa_ref) + ba_ref[...])
    i = jax.nn.sigmoid(_blockdiag(x16, wx_ref) + bx_ref[...])
    log_a = -LRU_C * r * jax.nn.softplus(-lam_ref[...])
    a = jnp.exp(log_a)
    one_minus_a2 = -jnp.tanh(log_a) * (a * a + 1.0)
    b = jnp.sqrt(one_minus_a2) * (i * xc)
    return a, b


LRU_HALO = 8


def _lru_prompt_kernel(x_ref, g_ref, cw_ref, cb_ref, wa_ref, ba_ref, wx_ref, bx_ref, lam_ref,
                       o_ref, st_ref, hl_ref, ext_ref, a_ref, b_ref, hs_ref, h_ref):
    i = pl.program_id(0)
    tr = x_ref.shape[0]
    pad = LRU_HALO - (LRU_CONV_WIDTH - 1)

    @pl.when(i == 0)
    def _():
        ext_ref[0:LRU_HALO, :] = jnp.zeros((LRU_HALO, D_LRU), f32)
        h_ref[...] = jnp.zeros_like(h_ref)

    ext_ref[LRU_HALO:LRU_HALO + tr, :] = x_ref[...]
    xc = jnp.broadcast_to(cb_ref[...], (tr, D_LRU))
    for w in range(LRU_CONV_WIDTH):
        xc = xc + ext_ref[pad + w:pad + w + tr, :] * cw_ref[w:w + 1, :]
    a, b = _lru_gates(xc, wa_ref, ba_ref, wx_ref, bx_ref, lam_ref)
    a_ref[...] = a
    b_ref[...] = b

    def body(t, h):
        h = a_ref[pl.ds(t, 1), :] * h + b_ref[pl.ds(t, 1), :]
        hs_ref[pl.ds(t, 1), :] = h
        return h

    h = lax.fori_loop(0, tr, body, h_ref[...], unroll=8)
    h_ref[...] = h
    hl_ref[...] = h
    o_ref[...] = (hs_ref[...] * jax.nn.silu(g_ref[...])).astype(bf16)
    tail = ext_ref[tr:tr + LRU_HALO, :]
    ext_ref[0:LRU_HALO, :] = tail
    st_ref[...] = ext_ref[pad:LRU_HALO, :]


def _lru_weight_specs(idx):
    vec = pl.BlockSpec((1, D_LRU), idx)
    blk = pl.BlockSpec((LRU_BLOCKS, LRU_BLOCK, LRU_BLOCK), lambda *a: (0, 0, 0))
    return [pl.BlockSpec((LRU_CONV_WIDTH, D_LRU), idx), vec, blk, vec, blk, vec, vec]


def _lru_prompt(z, lw):
    s_len = z.shape[0]
    tr = min(512, s_len)
    assert s_len % tr == 0
    return pl.pallas_call(
        _lru_prompt_kernel,
        grid=(s_len // tr,),
        in_specs=[pl.BlockSpec((tr, D_LRU), lambda i: (i, Z_LX)), pl.BlockSpec((tr, D_LRU), lambda i: (i, Z_GL))]
        + _lru_weight_specs(lambda i: (0, 0)),
        out_specs=[
            pl.BlockSpec((tr, D_LRU), lambda i: (i, 0)),
            pl.BlockSpec((LRU_CONV_WIDTH - 1, D_LRU), lambda i: (0, 0)),
            pl.BlockSpec((1, D_LRU), lambda i: (0, 0)),
        ],
        out_shape=[
            jax.ShapeDtypeStruct((s_len, D_LRU), bf16),
            jax.ShapeDtypeStruct((LRU_CONV_WIDTH - 1, D_LRU), f32),
            jax.ShapeDtypeStruct((1, D_LRU), f32),
        ],
        scratch_shapes=[
            pltpu.VMEM((tr + LRU_HALO, D_LRU), f32),
            pltpu.VMEM((tr, D_LRU), f32),
            pltpu.VMEM((tr, D_LRU), f32),
            pltpu.VMEM((tr, D_LRU), f32),
            pltpu.VMEM((1, D_LRU), f32),
        ],
        compiler_params=_cparams("arbitrary"),
        name="lru_prompt",
    )(z, z, *lw)


def _lru_sample_kernel(st_ref, x_ref, g_ref, h0_ref, cw_ref, cb_ref, wa_ref, ba_ref, wx_ref, bx_ref, lam_ref,
                       o_ref, nst_ref, hl_ref):
    n_st = st_ref.shape[0]
    t_len = x_ref.shape[0]
    bsz = x_ref.shape[1]

    def ext(r):
        return st_ref[r] if r < n_st else x_ref[r - n_st]

    h = h0_ref[...]
    for t in range(t_len):
        xc = jnp.broadcast_to(cb_ref[...], (bsz, D_LRU))
        for w in range(LRU_CONV_WIDTH):
            xc = xc + ext(t + w) * cw_ref[w:w + 1, :]
        a, b = _lru_gates(xc, wa_ref, ba_ref, wx_ref, bx_ref, lam_ref)
        h = a * h + b
        o_ref[t] = h * jax.nn.silu(g_ref[t])
    hl_ref[...] = h
    for r in range(n_st):
        nst_ref[r] = ext(r + t_len)


def _lru_sample(st_t, x_t, g_t, h0, lw):
    t_len, bsz, _ = x_t.shape
    return pl.pallas_call(
        _lru_sample_kernel,
        out_shape=[
            jax.ShapeDtypeStruct((t_len, bsz, D_LRU), f32),
            jax.ShapeDtypeStruct(st_t.shape, f32),
            jax.ShapeDtypeStruct((bsz, D_LRU), f32),
        ],
        compiler_params=pltpu.CompilerParams(vmem_limit_bytes=VMEM_LIMIT),
        name="lru_sample",
    )(st_t, x_t, g_t, h0, *lw)


MERGE_TM = 1024
MERGE_TN = 1024


def _merge_kernel(ya_ref, yc_ref, yl_ref, x_ref, w_ref, o_ref):
    w_rows = lambda lo, hi: w_ref[0, lo:hi, :].astype(bf16)
    acc = jnp.dot(ya_ref[...], w_rows(0, D_ATTN), preferred_element_type=f32)
    acc = acc + jnp.dot(yc_ref[...], w_rows(D_ATTN, D_ATTN + D_CONV), preferred_element_type=f32)
    acc = acc + jnp.dot(yl_ref[...], w_rows(D_ATTN + D_CONV, D_MODEL), preferred_element_type=f32)
    o_ref[...] = x_ref[...] + acc


def _merge(x2d, y_attn, y_conv, y_lru, w_out, layer):
    n = x2d.shape[0]
    tm = min(MERGE_TM, n)
    assert n % tm == 0
    return pl.pallas_call(
        _merge_kernel,
        grid=(n // tm, D_MODEL // MERGE_TN),
        in_specs=[
            pl.BlockSpec((tm, D_ATTN), lambda i, j: (i, 0)),
            pl.BlockSpec((tm, D_CONV), lambda i, j: (i, 0)),
            pl.BlockSpec((tm, D_LRU), lambda i, j: (i, 0)),
            pl.BlockSpec((tm, MERGE_TN), lambda i, j: (i, j)),
            pl.BlockSpec((1, D_MODEL, MERGE_TN), lambda i, j: (layer, 0, j)),
        ],
        out_specs=pl.BlockSpec((tm, MERGE_TN), lambda i, j: (i, j)),
        out_shape=jax.ShapeDtypeStruct((n, D_MODEL), f32),
        compiler_params=_cparams("arbitrary", "arbitrary"),
        name="merge",
    )(y_attn, y_conv, y_lru, x2d, w_out)


W_IN_SEGMENTS = (("q", D_ATTN), ("k", D_ATTN), ("v", D_ATTN), ("f", N_HEADS), ("ga", D_ATTN),
                 ("glu_a", D_CONV), ("glu_b", D_CONV), ("gc", D_CONV), ("lx", D_LRU), ("gl", D_LRU))
W_MAIN_ORDER = ("q", "k", "v", "ga", "gc", "lx", "gl", "glu_b", "glu_a")
D_IN = sum(size for _, size in W_IN_SEGMENTS)
PREP_TR = 256


def _prep_w_in_kernel(wt_ref, wm_ref, wf_ref):
    src = {}
    o = 0
    for name, size in W_IN_SEGMENTS:
        src[name] = (o, size)
        o += size
    d = 0
    for name in W_MAIN_ORDER:
        o, size = src[name]
        wm_ref[:, d:d + size] = wt_ref[0, o:o + size, :].T.astype(bf16)
        d += size
    o, size = src["f"]
    wf_ref[...] = jnp.zeros_like(wf_ref)
    wf_ref[:, 0:size] = wt_ref[0, o:o + size, :].T.astype(bf16)


def _prep_w_in(w_in_t, layer):
    assert all(sum(size for _, size in W_IN_SEGMENTS[:n]) % SUBLANE == 0 for n in range(len(W_IN_SEGMENTS)))
    return pl.pallas_call(
        _prep_w_in_kernel,
        grid=(D_MODEL // PREP_TR,),
        in_specs=[pl.BlockSpec((1, D_IN, PREP_TR), lambda i: (layer, 0, i))],
        out_specs=[pl.BlockSpec((PREP_TR, W_COLS), lambda i: (i, 0)), pl.BlockSpec((PREP_TR, LANE), lambda i: (i, 0))],
        out_shape=[jax.ShapeDtypeStruct((D_MODEL, W_COLS), bf16), jax.ShapeDtypeStruct((D_MODEL, LANE), bf16)],
        compiler_params=_cparams("arbitrary"),
        name="prep_w_in",
    )(w_in_t)


def kernel(x_prompt, x_sample, cache_k, cache_v, cache_logf, state_conv, state_lru_conv, state_lru_h,
           page_table, norm_g, w_in, b_f, q_norm_g, k_norm_g, conv_dw_w, conv_dw_b, conv_ln_g, conv_ln_b,
           conv_pw_w, lru_conv_w, lru_conv_b, lru_w_a, lru_b_a, lru_w_x, lru_b_x, lru_lambda, w_out):
    depth = w_in.shape[0]
    n_seq_p, s_len, _ = x_prompt.shape
    bsz, t_len, _ = x_sample.shape
    assert n_seq_p == 1, "the prompt kernels carry one sequence through their row tiles"
    n_pool = cache_k.shape[1]
    cache_k4 = cache_k.reshape(depth, n_pool, PAGE_SIZE * N_HEADS, HEAD_DIM)
    cache_v4 = cache_v.reshape(depth, n_pool, PAGE_SIZE * N_HEADS, HEAD_DIM)
    cache_lft = jnp.swapaxes(cache_logf, 2, 3)
    w_in_t = jnp.swapaxes(w_in, 1, 2)

    xp = x_prompt.reshape(s_len, D_MODEL)
    xs = x_sample.reshape(bsz * t_len, D_MODEL)
    outs = {k: [] for k in ("pk", "pv", "pf", "pc", "plc", "ph", "sk", "sv", "sf", "sc", "slc", "sh")}
    row = lambda v: v.reshape(1, -1)
    for l in range(depth):
        wm, wf = _prep_w_in(w_in_t, l)
        bfp = jnp.pad(b_f[l], (0, LANE - N_HEADS)).reshape(1, LANE)
        proj_w = (row(norm_g[l]), wm, wf, bfp, row(q_norm_g[l]), row(k_norm_g[l]))
        conv_w = (conv_dw_w[l], row(conv_dw_b[l]), row(conv_ln_g[l]), row(conv_ln_b[l]), conv_pw_w[l].astype(bf16))
        lru_w = (lru_conv_w[l], row(lru_conv_b[l]), lru_w_a[l].astype(bf16), row(lru_b_a[l]),
                 lru_w_x[l].astype(bf16), row(lru_b_x[l]), row(lru_lambda[l]))

        z, k_p, v_p, qkv16, logf, cs = _project(xp, *proj_w, seq_rows=s_len, carry_over=True)
        y_att = _attn_prompt(qkv16, cs[:, :N_HEADS].T.reshape(N_HEADS, 1, s_len), cs, z)
        y_conv, st_c = _conv_prompt(z, *conv_w)
        y_lru, st_l, h_l = _lru_prompt(z, lru_w)
        xp = _merge(xp, y_att, y_conv, y_lru, w_out, l)
        outs["pk"].append(k_p.reshape(1, s_len, N_HEADS, HEAD_DIM))
        outs["pv"].append(v_p.reshape(1, s_len, N_HEADS, HEAD_DIM))
        outs["pf"].append(logf[:, :N_HEADS].reshape(1, s_len, N_HEADS))
        outs["pc"].append(st_c[None])
        outs["plc"].append(st_l[None])
        outs["ph"].append(h_l)

        z, k_s, v_s, _, logf, cs = _project(xs, *proj_w, seq_rows=t_len, carry_over=False)
        z3 = z.reshape(bsz, t_len, Z_COLS)
        k3 = k_s.reshape(bsz, t_len, D_ATTN)
        v3 = v_s.reshape(bsz, t_len, D_ATTN)
        cs3 = cs[:, :N_HEADS].reshape(bsz, t_len, N_HEADS)
        cs_ht = jnp.swapaxes(cs3, 1, 2)
        cscol = jnp.pad(cs_ht, ((0, 0), (0, 0), (0, SUBLANE - t_len))).reshape(bsz, N_HEADS * SUBLANE, 1)
        csn = jnp.repeat(cs_ht, SUBLANE, axis=1)
        y_att = _attn_sample(z3, k3, v3, cscol, csn, cache_k4, cache_v4, cache_lft, page_table, l)
        tmaj = lambda a: jnp.swapaxes(a, 0, 1)
        zcol_t = lambda c: tmaj(z3[:, :, c * PROJ_TN:(c + 1) * PROJ_TN])
        bmaj16 = lambda a_t: tmaj(a_t).reshape(bsz * t_len, -1).astype(bf16)
        y_conv_t, nst_c = _conv_sample(tmaj(state_conv[l]), zcol_t(Z_U), zcol_t(Z_GC), *conv_w)
        y_lru_t, nst_l, h_l = _lru_sample(tmaj(state_lru_conv[l]), zcol_t(Z_LX), zcol_t(Z_GL), state_lru_h[l], lru_w)
        xs = _merge(xs, y_att.reshape(bsz * t_len, D_ATTN).astype(bf16), bmaj16(y_conv_t), bmaj16(y_lru_t), w_out, l)
        outs["sk"].append(k3.reshape(bsz, t_len, N_HEADS, HEAD_DIM))
        outs["sv"].append(v3.reshape(bsz, t_len, N_HEADS, HEAD_DIM))
        outs["sf"].append(logf[:, :N_HEADS].reshape(bsz, t_len, N_HEADS))
        outs["sc"].append(tmaj(nst_c))
        outs["slc"].append(tmaj(nst_l))
        outs["sh"].append(h_l)

    st = {k: jnp.stack(v) for k, v in outs.items()}
    return (xp.reshape(1, s_len, D_MODEL), xs.reshape(bsz, t_len, D_MODEL),
            st["pk"], st["pv"], st["pf"], st["pc"], st["plc"], st["ph"],
            st["sk"], st["sv"], st["sf"], st["sc"], st["slc"], st["sh"])
```
